```python
import math
import jax, jax.numpy as jnp
from jax import lax
import numpy as np

D_MODEL = 1024
BATCH = 32
SEQ = 2048
DEPTH = 1

CHUNK = 64
Q_BLOCK = 128
ROPE_THETA = 500000.0
NORM_EPS = 1e-6

A_HEADS = 8
A_HEAD_DIM = 64
IDX_HEADS = 8
IDX_DIM = 64
TOPK_KEYS = 256
B_HEADS = 4
B_QK_DIM = 64
B_V_DIM = 2 * B_QK_DIM
ROT_DIM = A_HEAD_DIM // 4
N_EXPERTS = 32
TOP_EXPERTS = 4
D_EXPERT = 1024
SWIGLU_LIMIT = 7.0
SWIGLU_ALPHA = 1.702
MOE_BLOCK = 128

SPLITS = (A_HEADS * A_HEAD_DIM,
          A_HEAD_DIM,
          A_HEAD_DIM,
          IDX_HEADS * IDX_DIM,
          IDX_DIM,
          IDX_HEADS,
          B_HEADS * 2 * B_QK_DIM,
          B_HEADS * 2 * B_QK_DIM,
          B_HEADS * B_V_DIM)
D_IN = sum(SPLITS)
SPLIT_OFFSETS = np.cumsum(SPLITS)[:-1].tolist()
D_MIX = A_HEADS * A_HEAD_DIM + B_HEADS * B_V_DIM

kernel_name = 'hybrid_dsa_diffattn_moe_block'


def rms_norm(x, g):
    x32 = x.astype(jnp.float32)
    y = x32 * lax.rsqrt(jnp.mean(x32 * x32, axis=-1, keepdims=True) + NORM_EPS)
    return (y * g.astype(jnp.float32)).astype(x.dtype)


def rope_tables(positions):
    inv_freq = ROPE_THETA ** (-jnp.arange(0, ROT_DIM, 2, dtype=jnp.float32) / ROT_DIM)
    ang = positions.astype(jnp.float32)[..., None] * inv_freq
    return jnp.cos(ang), jnp.sin(ang)


def partial_rope(x, cos, sin):
    half = ROT_DIM // 2
    c = cos[:, :, None, :]
    s = sin[:, :, None, :]
    x1 = x[..., :half].astype(jnp.float32)
    x2 = x[..., half:ROT_DIM].astype(jnp.float32)
    rot = jnp.concatenate([x1 * c - x2 * s, x2 * c + x1 * s], axis=-1).astype(x.dtype)
    return jnp.concatenate([rot, x[..., ROT_DIM:]], axis=-1)


def to_blocks(t):
    b, s = t.shape[:2]
    return jnp.moveaxis(t.reshape(b, s // Q_BLOCK, Q_BLOCK, *t.shape[2:]), 1, 0)


def from_blocks(t):
    nb, b, qb = t.shape[:3]
    return jnp.moveaxis(t, 0, 1).reshape(b, nb * qb, *t.shape[3:])


def dsa_mixer(q, k, v, qi, ki, wi, k_sel):
    s_len = q.shape[1]
    key_chunk = jnp.arange(s_len) // CHUNK
    kv = jnp.concatenate([k, v], axis=-1)

    def block(args):
        bi, qb, qib, wib = args
        q_chunk = (bi * Q_BLOCK + jnp.arange(Q_BLOCK)) // CHUNK
        admissible = key_chunk[None, :] <= q_chunk[:, None]
        rel = jax.nn.relu(jnp.einsum('bqhd,bsd->bqhs', qib, ki).astype(jnp.float32))
        score = jnp.einsum('bqhs,bqh->bqs', rel, wib.astype(jnp.float32))
        score = jnp.where(admissible[None], score, -jnp.inf)
        _, idx = lax.top_k(score, k_sel)
        valid = key_chunk[idx] <= q_chunk[None, :, None]
        kv_sel = jax.vmap(lambda kvb, ib: kvb[ib])(kv, idx)
        k_sel_, v_sel = kv_sel[..., :A_HEAD_DIM], kv_sel[..., A_HEAD_DIM:]
        logits = jnp.einsum('bqhd,bqkd->bqhk', qb, k_sel_).astype(jnp.float32) * (A_HEAD_DIM ** -0.5)
        logits = jnp.where(valid[:, :, None, :], logits, -jnp.inf)
        p = jax.nn.softmax(logits, axis=-1).astype(v.dtype)
        return jnp.einsum('bqhk,bqkd->bqhd', p, v_sel)

    nb = s_len // Q_BLOCK
    out = lax.map(block, (jnp.arange(nb), to_blocks(q), to_blocks(qi), to_blocks(wi)))
    return from_blocks(out)


def diff_mixer(q1, q2, k1, k2, v, lam, lam_init, g_subln):
    s_len = q1.shape[1]
    key_chunk = jnp.arange(s_len) // CHUNK
    scale = B_QK_DIM ** -0.5

    def block(args):
        bi, q1b, q2b = args
        q_chunk = (bi * Q_BLOCK + jnp.arange(Q_BLOCK)) // CHUNK
        mask = (key_chunk[None, :] <= q_chunk[:, None])[None, None]
        s1 = jnp.einsum('bqhd,bshd->bhqs', q1b, k1).astype(jnp.float32) * scale
        s2 = jnp.einsum('bqhd,bshd->bhqs', q2b, k2).astype(jnp.float32) * scale
        p1 = jax.nn.softmax(jnp.where(mask, s1, -jnp.inf), axis=-1)
        p2 = jax.nn.softmax(jnp.where(mask, s2, -jnp.inf), axis=-1)
        a = (p1 - lam * p2).astype(v.dtype)
        return jnp.einsum('bhqs,bshd->bqhd', a, v)

    nb = s_len // Q_BLOCK
    o = from_blocks(lax.map(block, (jnp.arange(nb), to_blocks(q1), to_blocks(q2))))
    return rms_norm(o, g_subln) * (1.0 - lam_init)


def moe(h, w_router, b_router, w_gu, b_gu, w_down, b_down):
    b, s, d = h.shape
    n = b * s
    hf = h.reshape(n, d)
    logits = (hf @ w_router + b_router).astype(jnp.float32)
    top_vals, top_idx = lax.top_k(logits, TOP_EXPERTS)
    gates = jax.nn.softmax(top_vals, axis=-1)

    nk = n * TOP_EXPERTS
    flat_e = top_idx.reshape(-1)
    flat_tok = jnp.repeat(jnp.arange(n, dtype=jnp.int32), TOP_EXPERTS)
    flat_g = gates.reshape(-1)
    order = jnp.argsort(flat_e)
    e_sorted = flat_e[order]
    tok_sorted = flat_tok[order]
    g_sorted = flat_g[order]
    counts = jnp.bincount(flat_e, length=N_EXPERTS)
    padded = ((counts + MOE_BLOCK - 1) // MOE_BLOCK) * MOE_BLOCK
    start = jnp.cumsum(counts) - counts
    pend = jnp.cumsum(padded)
    pstart = pend - padded
    dest = pstart[e_sorted] + (jnp.arange(nk) - start[e_sorted])
    n_blocks = -(-(nk + N_EXPERTS * (MOE_BLOCK - 1)) // MOE_BLOCK)
    m = n_blocks * MOE_BLOCK
    slot_tok = jnp.full((m,), n, dtype=jnp.int32).at[dest].set(tok_sorted)
    slot_g = jnp.zeros((m,), jnp.float32).at[dest].set(g_sorted)
    block_expert = jnp.minimum(
        jnp.searchsorted(pend, jnp.arange(n_blocks) * MOE_BLOCK, side='right'), N_EXPERTS - 1)
    h_pad = jnp.concatenate([hf, jnp.zeros((1, d), hf.dtype)], axis=0)

    def expert_block(args):
        tok, g, e = args
        xb = h_pad[tok]
        gu = xb @ w_gu[e] + b_gu[e]
        glu = jnp.minimum(gu[:, :D_EXPERT], SWIGLU_LIMIT)
        lin = jnp.clip(gu[:, D_EXPERT:], -SWIGLU_LIMIT, SWIGLU_LIMIT)
        act = glu * jax.nn.sigmoid(SWIGLU_ALPHA * glu) * (lin + 1.0)
        return (act @ w_down[e] + b_down[e]) * g[:, None].astype(h.dtype)

    ys = lax.map(expert_block, (slot_tok.reshape(n_blocks, MOE_BLOCK),
                                slot_g.reshape(n_blocks, MOE_BLOCK), block_expert))
    out = jax.ops.segment_sum(ys.reshape(m, d), slot_tok, num_segments=n + 1)[:n]
    return out.reshape(b, s, d).astype(h.dtype)


def setup_inputs(seed: int = 0) -> dict:
    key = jax.random.key(seed)
    ks = jax.random.split(key, 24)
    f32 = jnp.float32
    L = DEPTH
    nrm = lambda k, shape, scale: jax.random.normal(k, shape, f32) * scale
    gain = lambda k, shape: 1.0 + 0.02 * jax.random.normal(k, shape, f32)
    x = jax.random.normal(ks[0], (BATCH, SEQ, D_MODEL), f32)
    offsets = jax.random.randint(ks[1], (BATCH, 1), 0, 4096) * CHUNK
    positions = (offsets + jnp.arange(SEQ)[None, :]).astype(jnp.int32)
    return {
        'x': x,
        'positions': positions,
        'g_attn': gain(ks[2], (L, D_MODEL)),
        'w_in': nrm(ks[3], (L, D_MODEL, D_IN), D_MODEL ** -0.5),
        'g_qa': gain(ks[4], (L, A_HEAD_DIM)),
        'g_ka': gain(ks[5], (L, A_HEAD_DIM)),
        'g_idx_k': gain(ks[6], (L, IDX_DIM)),
        'lambda_q1': nrm(ks[7], (L, B_QK_DIM), 0.1),
        'lambda_k1': nrm(ks[8], (L, B_QK_DIM), 0.1),
        'lambda_q2': nrm(ks[9], (L, B_QK_DIM), 0.1),
        'lambda_k2': nrm(ks[10], (L, B_QK_DIM), 0.1),
        'g_qb': gain(ks[11], (L, B_QK_DIM)),
        'g_kb': gain(ks[12], (L, B_QK_DIM)),
        'g_subln': gain(ks[13], (L, B_V_DIM)),
        'w_o': nrm(ks[14], (L, D_MIX, D_MODEL), D_MIX ** -0.5),
        'g_ffn': gain(ks[15], (L, D_MODEL)),
        'w_router': nrm(ks[16], (L, D_MODEL, N_EXPERTS), D_MODEL ** -0.5),
        'b_router': nrm(ks[17], (L, N_EXPERTS), 0.01),
        'w_gu': nrm(ks[18], (L, N_EXPERTS, D_MODEL, 2 * D_EXPERT), D_MODEL ** -0.5),
        'b_gu': nrm(ks[19], (L, N_EXPERTS, 2 * D_EXPERT), 0.02),
        'w_down': nrm(ks[20], (L, N_EXPERTS, D_EXPERT, D_MODEL), D_EXPERT ** -0.5),
        'b_down': nrm(ks[21], (L, N_EXPERTS, D_MODEL), 0.02),
    }


def reference(x, positions, g_attn, w_in, g_qa, g_ka, g_idx_k, lambda_q1, lambda_k1,
              lambda_q2, lambda_k2, g_qb, g_kb, g_subln, w_o, g_ffn, w_router, b_router,
              w_gu, b_gu, w_down, b_down):
    b, s, _ = x.shape
    k_sel = min(TOPK_KEYS, s // 4)
    cos, sin = rope_tables(positions)
    for layer in range(DEPTH):
        lam_init = 0.8 - 0.6 * math.exp(-0.3 * layer)
        h = rms_norm(x, g_attn[layer])
        proj = h @ w_in[layer]
        aq, ak, av, iq, ik, iw, bq, bk, bv = jnp.split(proj, SPLIT_OFFSETS, axis=-1)

        aq = partial_rope(rms_norm(aq.reshape(b, s, A_HEADS, A_HEAD_DIM), g_qa[layer]), cos, sin)
        ak = partial_rope(rms_norm(ak, g_ka[layer])[:, :, None], cos, sin)[:, :, 0]
        iq = partial_rope(iq.reshape(b, s, IDX_HEADS, IDX_DIM), cos, sin)
        ik = partial_rope(rms_norm(ik, g_idx_k[layer])[:, :, None], cos, sin)[:, :, 0]
        iw = iw * ((IDX_HEADS * IDX_DIM) ** -0.5)
        y_a = dsa_mixer(aq, ak, av, iq, ik, iw, k_sel).reshape(b, s, A_HEADS * A_HEAD_DIM)

        bq = bq.reshape(b, s, B_HEADS, 2, B_QK_DIM)
        bk = bk.reshape(b, s, B_HEADS, 2, B_QK_DIM)
        q1 = partial_rope(rms_norm(bq[:, :, :, 0], g_qb[layer]), cos, sin)
        q2 = partial_rope(rms_norm(bq[:, :, :, 1], g_qb[layer]), cos, sin)
        k1 = partial_rope(rms_norm(bk[:, :, :, 0], g_kb[layer]), cos, sin)
        k2 = partial_rope(rms_norm(bk[:, :, :, 1], g_kb[layer]), cos, sin)
        bv = bv.reshape(b, s, B_HEADS, B_V_DIM)
        lam = (jnp.exp(jnp.sum(lambda_q1[layer].astype(jnp.float32) * lambda_k1[layer].astype(jnp.float32)))
               - jnp.exp(jnp.sum(lambda_q2[layer].astype(jnp.float32) * lambda_k2[layer].astype(jnp.float32)))
               + lam_init)
        y_b = diff_mixer(q1, q2, k1, k2, bv, lam, lam_init, g_subln[layer]).reshape(b, s, B_HEADS * B_V_DIM)

        x = x + jnp.concatenate([y_a, y_b], axis=-1) @ w_o[layer]
        x = x + moe(rms_norm(x, g_ffn[layer]), w_router[layer], b_router[layer],
                    w_gu[layer], b_gu[layer], w_down[layer], b_down[layer])
    return x
```

```python
import functools
import math

import jax
import jax.numpy as jnp
import numpy as np
from jax import lax
from jax.experimental import pallas as pl
from jax.experimental.pallas import tpu as pltpu

CHUNK = 64
ROPE_THETA = 500000.0
NORM_EPS = 1e-6
A_HEADS = 8
HEAD_DIM = 64
IDX_HEADS = 8
TOPK_KEYS = 256
B_HEADS = 4
B_V_DIM = 128
ROT_DIM = HEAD_DIM // 4
ROT_HALF = ROT_DIM // 2
N_EXPERTS = 32
TOP_EXPERTS = 4
SWIGLU_LIMIT = 7.0
SWIGLU_ALPHA = 1.702

LANES = 128
VMEM_LIMIT = 48 * 1024 * 1024

PROJ_ROWS = 512
DIFF_Q = 256
DIFF_K = 512
DSA_Q = 128
DSA_K = 512
MOE_ROWS = 256
COMBINE_ROWS = 512

MASKED = -1e30
INT_MIN = -(2 ** 31)

BF16 = jnp.bfloat16
F32 = jnp.float32
NT_DIMS = (((1,), (1,)), ((), ()))


def _params(*sem):
    return pltpu.CompilerParams(dimension_semantics=sem, vmem_limit_bytes=VMEM_LIMIT)


def _rope_kernel(invf_ref, pos_ref, cos_ref, sin_ref):
    f = pl.program_id(0)
    ang = pos_ref[...].astype(F32) * invf_ref[f]
    cos_ref[0] = jnp.cos(ang)
    sin_ref[0] = jnp.sin(ang)


def _rope_tables(positions):
    b, s = positions.shape
    inv_freq = ROPE_THETA ** (-jnp.arange(0, ROT_DIM, 2, dtype=F32) / ROT_DIM)
    cos_t, sin_t = pl.pallas_call(
        _rope_kernel,
        out_shape=(jax.ShapeDtypeStruct((ROT_HALF, b, s), F32),) * 2,
        grid=(ROT_HALF,),
        in_specs=[pl.BlockSpec(memory_space=pltpu.SMEM),
                  pl.BlockSpec((b, s), lambda f: (0, 0))],
        out_specs=(pl.BlockSpec((1, b, s), lambda f: (f, 0, 0)),) * 2,
        compiler_params=_params("arbitrary"),
        name="rope_tables",
    )(inv_freq, positions)
    n = b * s
    cos8 = jnp.transpose(cos_t, (1, 2, 0)).reshape(n, ROT_HALF)
    sin8 = jnp.transpose(sin_t, (1, 2, 0)).reshape(n, ROT_HALF)
    ones = jnp.ones((n, HEAD_DIM - ROT_DIM), F32)
    zeros = jnp.zeros((n, HEAD_DIM - ROT_DIM), F32)
    c_tab = jnp.concatenate([cos8, cos8, ones], axis=1)
    s_tab = jnp.concatenate([-sin8, sin8, zeros], axis=1)
    return jnp.tile(c_tab, (1, 2)), jnp.tile(s_tab, (1, 2))


def _segment_rms(y, gain):
    lane = lax.broadcasted_iota(jnp.int32, (1, LANES), 1)
    lo = lane < HEAD_DIM
    y2 = y * y
    s0 = jnp.sum(jnp.where(lo, y2, 0.0), axis=-1, keepdims=True)
    s1 = jnp.sum(jnp.where(lo, 0.0, y2), axis=-1, keepdims=True)
    ms = jnp.where(lo, s0, s1) * (1.0 / HEAD_DIM)
    return y * lax.rsqrt(ms + NORM_EPS) * gain


def _rope128(y, c_tab, s_tab):
    lane = lax.broadcasted_iota(jnp.int32, (1, LANES), 1)
    first = (lane % HEAD_DIM) < ROT_HALF
    up = pltpu.roll(y, LANES - ROT_HALF, 1)
    down = pltpu.roll(y, ROT_HALF, 1)
    return y * c_tab + jnp.where(first, up, down) * s_tab


def _in_proj_kernel(x_ref, g_ref, w_ref, c_ref, s_ref, gains_ref,
                    aq_ref, iq_ref, bq_ref, bk_ref, bv_ref, kka_ref, kki_ref, vv_ref, iw_ref):
    x = x_ref[...]
    ms = jnp.mean(x * x, axis=-1, keepdims=True)
    h = (x * lax.rsqrt(ms + NORM_EPS) * g_ref[...]).astype(BF16)
    c_tab = c_ref[...]
    s_tab = s_ref[...]
    wide = 4 * LANES

    def group(col, out_ref, gain_row, rope):
        acc = jnp.dot(h, w_ref[:, col:col + wide], preferred_element_type=F32)
        for t in range(4):
            y = acc[:, t * LANES:(t + 1) * LANES]
            if gain_row is not None:
                y = _segment_rms(y, gains_ref[gain_row:gain_row + 1, :])
            if rope:
                y = _rope128(y, c_tab, s_tab)
            out_ref[:, t * LANES:(t + 1) * LANES] = y.astype(out_ref.dtype)

    group(0 * wide, aq_ref, 0, True)
    group(1 * wide, iq_ref, None, True)
    group(2 * wide, bq_ref, 1, True)
    group(3 * wide, bk_ref, 2, True)
    group(4 * wide, bv_ref, None, False)
    acc = jnp.dot(h, w_ref[:, 5 * wide:6 * wide], preferred_element_type=F32)
    ka = _rope128(_segment_rms(acc[:, 0:LANES], gains_ref[3:4, :]), c_tab, s_tab)
    ki = _rope128(_segment_rms(acc[:, LANES:2 * LANES], gains_ref[4:5, :]), c_tab, s_tab)
    kka_ref[...] = ka.astype(BF16)
    kki_ref[...] = ki.astype(BF16)
    vv_ref[...] = acc[:, 2 * LANES:3 * LANES].astype(BF16)
    iw_ref[...] = acc[:, 3 * LANES:4 * LANES] * ((IDX_HEADS * HEAD_DIM) ** -0.5)


def _in_proj(x2, g_attn, w_in, c_tab, s_tab, g_qa, g_ka, g_idx_k, g_qb, g_kb):
    n, d = x2.shape
    ha = A_HEADS * HEAD_DIM
    o = np.cumsum([0, ha, HEAD_DIM, HEAD_DIM, IDX_HEADS * HEAD_DIM, HEAD_DIM, IDX_HEADS,
                   B_HEADS * 2 * HEAD_DIM, B_HEADS * 2 * HEAD_DIM, B_HEADS * B_V_DIM])
    aq, ak, av, iq, ik, iw, bq, bk, bv = [w_in[:, o[i]:o[i + 1]] for i in range(9)]
    pad = jnp.zeros((d, LANES - IDX_HEADS), w_in.dtype)
    w = jnp.concatenate([aq, iq, bq, bk, bv, ak, ak, ik, ik, av, av, iw, pad], axis=1).astype(BF16)
    two = lambda g: jnp.tile(g.reshape(1, HEAD_DIM), (1, 2))
    gains = jnp.concatenate([two(g_qa), two(g_qb), two(g_kb), two(g_ka), two(g_idx_k),
                             jnp.zeros((3, LANES), F32)], axis=0)
    tm = min(PROJ_ROWS, n)
    wide = 4 * LANES
    row = lambda width: pl.BlockSpec((tm, width), lambda i: (i, 0))
    full = lambda a: pl.BlockSpec(a.shape, lambda i: (0, 0))
    outs = pl.pallas_call(
        _in_proj_kernel,
        out_shape=(jax.ShapeDtypeStruct((n, wide), BF16),) * 5
        + (jax.ShapeDtypeStruct((n, LANES), BF16),) * 3
        + (jax.ShapeDtypeStruct((n, LANES), F32),),
        grid=(n // tm,),
        in_specs=[row(d), full(g_attn), full(w), row(LANES), row(LANES), full(gains)],
        out_specs=(row(wide),) * 5 + (row(LANES),) * 4,
        compiler_params=_params("arbitrary"),
        name="in_proj",
    )(x2, g_attn, w, c_tab, s_tab, gains)
    return outs


def _diff_attn_kernel(lam_ref, q_ref, k_ref, v_ref, g_ref, o_ref, *, lam_init):
    i = pl.program_id(1)
    tq = q_ref.shape[1]
    tk = DIFF_K
    lam = lam_ref[0]
    nkb = (i * tq + tq + tk - 1) // tk
    scale = HEAD_DIM ** -0.5
    lane = lax.broadcasted_iota(jnp.int32, (1, LANES), 1)
    lo = lane < HEAD_DIM
    q_chunk = (i * tq + lax.broadcasted_iota(jnp.int32, (2 * tq, 1), 0) % tq) // CHUNK
    k_chunk0 = lax.broadcasted_iota(jnp.int32, (1, tk), 1) // CHUNK

    for hd in range(B_HEADS):
        cols = slice(hd * LANES, (hd + 1) * LANES)
        qh = q_ref[0, :, cols]
        zero = jnp.zeros_like(qh)
        qs = jnp.concatenate([jnp.where(lo, qh, zero), jnp.where(lo, zero, qh)], axis=0)

        def body(j, carry):
            m, l, acc = carry
            start = pl.multiple_of(j * tk, tk)
            kj = k_ref[0, pl.ds(start, tk), cols]
            vj = v_ref[0, pl.ds(start, tk), cols]
            s = lax.dot_general(qs, kj, NT_DIMS, preferred_element_type=F32) * scale
            s = jnp.where(k_chunk0 + j * (tk // CHUNK) <= q_chunk, s, MASKED)
            m_new = jnp.maximum(m, jnp.max(s, axis=-1, keepdims=True))
            alpha = jnp.exp(m - m_new)
            p = jnp.exp(s - m_new)
            l = alpha * l + jnp.sum(p, axis=-1, keepdims=True)
            pv = jnp.concatenate(
                [jnp.dot(p[:tq].astype(BF16), vj, preferred_element_type=F32),
                 jnp.dot(p[tq:].astype(BF16), vj, preferred_element_type=F32)], axis=0)
            return m_new, l, alpha * acc + pv

        init = (jnp.full((2 * tq, 1), MASKED, F32), jnp.zeros((2 * tq, 1), F32),
                jnp.zeros((2 * tq, LANES), F32))
        _, l, acc = lax.fori_loop(0, nkb, body, init)
        o = acc / l
        o = o[:tq] - lam * o[tq:]
        ms = jnp.mean(o * o, axis=-1, keepdims=True)
        o = o * lax.rsqrt(ms + NORM_EPS) * g_ref[...] * (1.0 - lam_init)
        o_ref[0, :, cols] = o.astype(o_ref.dtype)


def _diff_attn(bq, bk, bv, lam, g_subln, lam_init):
    b, s, w = bq.shape
    tq = min(DIFF_Q, s)
    kv = pl.BlockSpec((1, s, w), lambda bi, i: (bi, 0, 0))
    qo = pl.BlockSpec((1, tq, w), lambda bi, i: (bi, i, 0))
    return pl.pallas_call(
        functools.partial(_diff_attn_kernel, lam_init=lam_init),
        out_shape=jax.ShapeDtypeStruct((b, s, w), BF16),
        grid=(b, s // tq),
        in_specs=[pl.BlockSpec(memory_space=pltpu.SMEM), qo, kv, kv,
                  pl.BlockSpec((1, B_V_DIM), lambda bi, i: (0, 0))],
        out_specs=qo,
        compiler_params=_params("arbitrary", "arbitrary"),
        name="diff_attn",
    )(lam, bq, bk, bv, g_subln)


def _stack_heads(src_ref, dst_ref):
    tq = src_ref.shape[1]
    lane = lax.broadcasted_iota(jnp.int32, (1, LANES), 1)
    lo = lane < HEAD_DIM
    for hd in range(A_HEADS):
        blk = src_ref[0, :, (hd // 2) * LANES:(hd // 2 + 1) * LANES]
        keep = lo if hd % 2 == 0 else jnp.logical_not(lo)
        dst_ref[hd * tq:(hd + 1) * tq, :] = jnp.where(keep, blk, jnp.zeros_like(blk))


def _dsa_attn_kernel(aq_ref, iq_ref, kka_ref, kki_ref, vv_ref, iw_ref, o_ref,
                     qa_ref, qi_ref, key_ref, bias_ref, *, k_sel):
    i = pl.program_id(1)
    tq = aq_ref.shape[1]
    nsel = i + 1
    groups = DSA_K // tq
    nkb = (nsel + groups - 1) // groups
    _stack_heads(aq_ref, qa_ref)
    _stack_heads(iq_ref, qi_ref)
    sub = lax.broadcasted_iota(jnp.int32, (tq, tq), 0)
    qcol = lax.broadcasted_iota(jnp.int32, (tq, tq), 1)
    diag_masked = (sub // CHUNK) > (qcol // CHUNK)

    def score_block(j, _):
        start = pl.multiple_of(j * tq, tq)
        kj = kki_ref[0, pl.ds(start, tq), :]
        st = lax.dot_general(kj, qi_ref[...], NT_DIMS, preferred_element_type=F32)
        score = jnp.zeros((tq, tq), F32)
        for hd in range(IDX_HEADS):
            rel = jnp.maximum(st[:, hd * tq:(hd + 1) * tq], 0.0)
            score = score + rel * iw_ref[0, hd:hd + 1, :]
        bits = pltpu.bitcast(score, jnp.int32)
        key = jnp.where(bits < 0, jnp.int32(INT_MIN) - bits, bits)
        key = jnp.where(jnp.logical_and(j == i, diag_masked), jnp.int32(INT_MIN), key)
        key_ref[j] = key
        return 0

    lax.fori_loop(0, nsel, score_block, 0)

    def count(pred):
        def body(j, acc):
            hit = jnp.where(pred(key_ref[j], j), 1, 0).astype(jnp.int32)
            return acc + jnp.sum(hit.reshape(tq // 8, 8, tq), axis=0)
        acc = lax.fori_loop(0, nsel, body, jnp.zeros((8, tq), jnp.int32))
        return jnp.sum(acc, axis=0, keepdims=True)

    def search_bit(t, thr):
        cand = thr + lax.shift_left(jnp.int32(1), 31 - t)
        cnt = count(lambda key, j: key >= cand)
        return jnp.where(cnt >= k_sel, cand, thr)

    thr = lax.fori_loop(0, 32, search_bit, jnp.full((1, tq), INT_MIN, jnp.int32))
    n_gt = count(lambda key, j: key > thr)
    n_eq = count(lambda key, j: key == thr)
    need = k_sel - n_gt
    tie_break = jnp.logical_and(thr > INT_MIN, n_eq > need)

    def tie_search(_):
        def bit(t, cut):
            cand = cut + lax.shift_left(jnp.int32(1), 15 - t)
            below = count(lambda key, j: jnp.logical_and(key == thr, sub + j * tq < cand))
            return jnp.where(below < need, cand, cut)
        return lax.fori_loop(0, 16, bit, jnp.zeros((1, tq), jnp.int32))

    any_tie = jnp.max(jnp.where(tie_break, 1, 0)) > 0
    cut = lax.cond(any_tie, tie_search, lambda _: jnp.full((1, tq), 2 ** 30, jnp.int32), 0)
    cut = jnp.where(tie_break, cut, 2 ** 30)

    def bias_block(j, _):
        key = key_ref[j]
        sel = jnp.logical_or(key > thr, jnp.logical_and(key == thr, sub + j * tq <= cut))
        sel = jnp.logical_and(sel, key > INT_MIN)
        bias_ref[j] = jnp.where(sel, 0.0, MASKED).astype(F32).T
        return 0

    lax.fori_loop(0, nsel, bias_block, 0)

    def fill_block(j, _):
        bias_ref[j] = jnp.full((tq, tq), MASKED, F32)
        return 0

    lax.fori_loop(nsel, nkb * groups, fill_block, 0)

    scale = HEAD_DIM ** -0.5
    rows = A_HEADS * tq

    def attn_block(jb, carry):
        m, l, acc = carry
        start = pl.multiple_of(jb * DSA_K, DSA_K)
        kj = kka_ref[0, pl.ds(start, DSA_K), :]
        vj = vv_ref[0, pl.ds(start, DSA_K), :]
        bias = jnp.concatenate([bias_ref[jb * groups + t] for t in range(groups)], axis=1)
        s = lax.dot_general(qa_ref[...], kj, NT_DIMS, preferred_element_type=F32) * scale
        s = (s.reshape(A_HEADS, tq, DSA_K) + bias[None]).reshape(rows, DSA_K)
        m_new = jnp.maximum(m, jnp.max(s, axis=-1, keepdims=True))
        alpha = jnp.exp(m - m_new)
        p = jnp.exp(s - m_new)
        l = alpha * l + jnp.sum(p, axis=-1, keepdims=True)
        acc = alpha * acc + jnp.dot(p.astype(BF16), vj, preferred_element_type=F32)
        return m_new, l, acc

    init = (jnp.full((rows, 1), MASKED, F32), jnp.zeros((rows, 1), F32),
            jnp.zeros((rows, LANES), F32))
    _, l, acc = lax.fori_loop(0, nkb, attn_block, init)
    out = acc / l
    lane = lax.broadcasted_iota(jnp.int32, (1, LANES), 1)
    lo = lane < HEAD_DIM
    for g in range(A_HEADS // 2):
        even = out[(2 * g) * tq:(2 * g + 1) * tq]
        odd = out[(2 * g + 1) * tq:(2 * g + 2) * tq]
        o_ref[0, :, g * LANES:(g + 1) * LANES] = jnp.where(lo, even, odd).astype(o_ref.dtype)


def _dsa_attn(aq, iq, kka, kki, vv, iw_t, k_sel):
    b, s, w = aq.shape
    tq = min(DSA_Q, s)
    nblk = s // tq
    qo = pl.BlockSpec((1, tq, w), lambda bi, i: (bi, i, 0))
    kv = pl.BlockSpec((1, s, LANES), lambda bi, i: (bi, 0, 0))
    return pl.pallas_call(
        functools.partial(_dsa_attn_kernel, k_sel=k_sel),
        out_shape=jax.ShapeDtypeStruct((b, s, w), BF16),
        grid=(b, nblk),
        in_specs=[qo, qo, kv, kv, kv,
                  pl.BlockSpec((1, IDX_HEADS, tq), lambda bi, i: (bi, 0, i))],
        out_specs=qo,
        scratch_shapes=[pltpu.VMEM((A_HEADS * tq, LANES), BF16),
                        pltpu.VMEM((IDX_HEADS * tq, LANES), BF16),
                        pltpu.VMEM((nblk, tq, tq), jnp.int32),
                        pltpu.VMEM((nblk + DSA_K // tq, tq, tq), F32)],
        compiler_params=_params("arbitrary", "arbitrary"),
        name="dsa_attn",
    )(aq, iq, kka, kki, vv, iw_t)


def _pack_halves(y):
    w = y.shape[1] // 2
    r = y.astype(BF16).astype(F32)
    lo = pltpu.bitcast(r[:, :w], jnp.uint32)
    hi = pltpu.bitcast(r[:, w:], jnp.uint32)
    return jnp.bitwise_or(jnp.bitwise_and(hi, jnp.uint32(0xFFFF0000)), lax.shift_right_logical(lo, jnp.uint32(16)))


def _unpack_halves(u):
    lo = pltpu.bitcast(lax.shift_left(u, jnp.uint32(16)), F32)
    hi = pltpu.bitcast(jnp.bitwise_and(u, jnp.uint32(0xFFFF0000)), F32)
    return lo, hi


def _out_proj_kernel(x_ref, ya_ref, yb_ref, wa_ref, wb_ref, g_ref, wr_ref, br_ref,
                     x1_ref, hp_ref, idx_ref, gate_ref):
    x1 = (x_ref[...] + jnp.dot(ya_ref[...], wa_ref[...], preferred_element_type=F32)
          + jnp.dot(yb_ref[...], wb_ref[...], preferred_element_type=F32))
    x1_ref[...] = x1
    ms = jnp.mean(x1 * x1, axis=-1, keepdims=True)
    h = x1 * lax.rsqrt(ms + NORM_EPS) * g_ref[...]
    hp_ref[...] = _pack_halves(h)
    logits = jnp.dot(h.astype(BF16), wr_ref[...], preferred_element_type=F32) + br_ref[...]
    lane = lax.broadcasted_iota(jnp.int32, logits.shape, 1)
    work = jnp.where(lane < N_EXPERTS, logits, -jnp.inf)
    idx_out = jnp.zeros(logits.shape, jnp.int32)
    val_out = jnp.full(logits.shape, -jnp.inf, F32)
    for r in range(TOP_EXPERTS):
        top = jnp.max(work, axis=-1, keepdims=True)
        arg = jnp.min(jnp.where(work == top, lane, LANES), axis=-1, keepdims=True)
        idx_out = jnp.where(lane == r, arg, idx_out)
        val_out = jnp.where(lane == r, top, val_out)
        work = jnp.where(lane == arg, -jnp.inf, work)
    e = jnp.exp(val_out - jnp.max(val_out, axis=-1, keepdims=True))
    gate_ref[...] = e / jnp.sum(e, axis=-1, keepdims=True)
    idx_ref[...] = idx_out


def _out_proj(x2, y_a, y_b, w_o, g_ffn, w_router, b_router):
    n, d = x2.shape
    da = y_a.shape[1]
    wa = w_o[:da].astype(BF16)
    wb = w_o[da:].astype(BF16)
    wr = jnp.concatenate([w_router, jnp.zeros((d, LANES - N_EXPERTS), w_router.dtype)], 1).astype(BF16)
    br = jnp.concatenate([b_router, jnp.zeros((LANES - N_EXPERTS,), F32)]).reshape(1, LANES)
    tm = min(PROJ_ROWS, n)
    row = lambda width: pl.BlockSpec((tm, width), lambda i: (i, 0))
    full = lambda a: pl.BlockSpec(a.shape, lambda i: (0, 0))
    return pl.pallas_call(
        _out_proj_kernel,
        out_shape=(jax.ShapeDtypeStruct((n, d), F32), jax.ShapeDtypeStruct((n, d // 2), jnp.uint32),
                   jax.ShapeDtypeStruct((n, LANES), jnp.int32), jax.ShapeDtypeStruct((n, LANES), F32)),
        grid=(n // tm,),
        in_specs=[row(d), row(da), row(y_b.shape[1]), full(wa), full(wb), full(g_ffn), full(wr), full(br)],
        out_specs=(row(d), row(d // 2), row(LANES), row(LANES)),
        compiler_params=_params("arbitrary"),
        name="out_proj",
    )(x2, y_a, y_b, wa, wb, g_ffn, wr, br)


def _moe_kernel(bexp_ref, nused_ref, tok_ref, dst_ref, gate_ref, h_hbm, wgu_ref, bgu_ref,
                wd_ref, bd_ref, y_hbm, xbuf, ybuf, gsem, ssem):
    blk = pl.program_id(0)
    rows = xbuf.shape[0]
    half = xbuf.shape[1]
    f = wd_ref.shape[1]

    def gather_copy(r):
        return pltpu.make_async_copy(h_hbm.at[pl.ds(tok_ref[0, 0, r], 1)], xbuf.at[pl.ds(r, 1)], gsem)

    def scatter_copy(r):
        return pltpu.make_async_copy(ybuf.at[pl.ds(r, 1)], y_hbm.at[pl.ds(dst_ref[0, 0, r], 1)], ssem)

    def start_gather(r, _):
        gather_copy(r).start()
        return 0

    def wait_gather(r, _):
        gather_copy(r).wait()
        return 0

    lax.fori_loop(0, rows, start_gather, 0)
    lax.fori_loop(0, rows, wait_gather, 0)

    @pl.when(blk < nused_ref[0])
    def _():
        lo, hi = _unpack_halves(xbuf[...])
        gu = (jnp.dot(lo.astype(BF16), wgu_ref[0, :half, :], preferred_element_type=F32)
              + jnp.dot(hi.astype(BF16), wgu_ref[0, half:, :], preferred_element_type=F32)
              + bgu_ref[0])
        glu = jnp.minimum(gu[:, :f], SWIGLU_LIMIT)
        lin = jnp.clip(gu[:, f:], -SWIGLU_LIMIT, SWIGLU_LIMIT)
        act = glu * jax.nn.sigmoid(SWIGLU_ALPHA * glu) * (lin + 1.0)
        y = jnp.dot(act.astype(BF16), wd_ref[0], preferred_element_type=F32) + bd_ref[0]
        ybuf[...] = _pack_halves(y * gate_ref[0])

    @pl.when(blk >= nused_ref[0])
    def _():
        ybuf[...] = jnp.zeros(ybuf.shape, ybuf.dtype)

    def start_scatter(r, _):
        scatter_copy(r).start()
        return 0

    def wait_scatter(r, _):
        scatter_copy(r).wait()
        return 0

    lax.fori_loop(0, rows, start_scatter, 0)
    lax.fori_loop(0, rows, wait_scatter, 0)


def _moe_experts(hp, top_idx, gates, w_gu, b_gu, w_down, b_down):
    n, half = hp.shape
    d = 2 * half
    f = w_down.shape[1]
    nk = n * TOP_EXPERTS
    rows = MOE_ROWS
    n_blocks = -(-(nk + N_EXPERTS * (rows - 1)) // rows)
    m = n_blocks * rows

    flat_e = top_idx.reshape(-1)
    flat_tok = jnp.repeat(jnp.arange(n, dtype=jnp.int32), TOP_EXPERTS)
    flat_k = jnp.tile(jnp.arange(TOP_EXPERTS, dtype=jnp.int32), n)
    onehot = (flat_e[:, None] == jnp.arange(N_EXPERTS, dtype=jnp.int32)[None, :]).astype(jnp.int32)
    csum = jnp.cumsum(onehot, axis=0)
    rank = jnp.take_along_axis(csum, flat_e[:, None], axis=1)[:, 0] - 1
    counts = csum[-1]
    padded = ((counts + rows - 1) // rows) * rows
    pend = jnp.cumsum(padded)
    pstart = pend - padded
    dest = pstart[flat_e] + rank
    slot_tok = jnp.zeros((m,), jnp.int32).at[dest].set(flat_tok)
    slot_gate = jnp.zeros((m,), F32).at[dest].set(gates.reshape(-1))
    is_real = jnp.zeros((m,), jnp.int32).at[dest].set(1)
    pad_row = nk + jnp.cumsum(1 - is_real) - 1
    slot_dst = jnp.where(is_real > 0, jnp.zeros((m,), jnp.int32).at[dest].set(flat_k * n + flat_tok),
                         pad_row.astype(jnp.int32))
    block_expert = jnp.minimum(
        jnp.searchsorted(pend, jnp.arange(n_blocks, dtype=jnp.int32) * rows, side='right'),
        N_EXPERTS - 1).astype(jnp.int32)
    n_used = (pend[-1] // rows).astype(jnp.int32).reshape(1)

    wgu = w_gu.astype(BF16)
    wd = w_down.astype(BF16)
    bgu = b_gu.reshape(N_EXPERTS, 1, 2 * f)
    bd = b_down.reshape(N_EXPERTS, 1, d)
    smem_blk = pl.BlockSpec((1, 1, rows), lambda i, be, nu: (i, 0, 0), memory_space=pltpu.SMEM)
    by_expert = lambda shape: pl.BlockSpec((1,) + shape, lambda i, be, nu: (be[i], 0, 0))
    y_rows = pl.pallas_call(
        _moe_kernel,
        out_shape=jax.ShapeDtypeStruct((m, half), jnp.uint32),
        grid_spec=pltpu.PrefetchScalarGridSpec(
            num_scalar_prefetch=2,
            grid=(n_blocks,),
            in_specs=[smem_blk, smem_blk,
                      pl.BlockSpec((1, rows, 1), lambda i, be, nu: (i, 0, 0)),
                      pl.BlockSpec(memory_space=pl.ANY),
                      by_expert((d, 2 * f)), by_expert((1, 2 * f)),
                      by_expert((f, d)), by_expert((1, d))],
            out_specs=pl.BlockSpec(memory_space=pl.ANY),
            scratch_shapes=[pltpu.VMEM((rows, half), jnp.uint32), pltpu.VMEM((rows, half), jnp.uint32),
                            pltpu.SemaphoreType.DMA, pltpu.SemaphoreType.DMA]),
        compiler_params=_params("arbitrary"),
        name="moe_experts",
    )(block_expert, n_used, slot_tok.reshape(n_blocks, 1, rows), slot_dst.reshape(n_blocks, 1, rows),
      slot_gate.reshape(n_blocks, rows, 1), hp, wgu, bgu, wd, bd)
    return y_rows


def _combine_kernel(x_ref, y0_ref, y1_ref, y2_ref, y3_ref, o_ref):
    half = y0_ref.shape[1]
    lo = x_ref[:, :half]
    hi = x_ref[:, half:]
    for y_ref in (y0_ref, y1_ref, y2_ref, y3_ref):
        a, b = _unpack_halves(y_ref[...])
        lo = lo + a
        hi = hi + b
    o_ref[:, :half] = lo
    o_ref[:, half:] = hi


def _moe_combine(x1, y_rows):
    n, d = x1.shape
    tm = min(COMBINE_ROWS, n)
    nt = n // tm
    slot = lambda k: pl.BlockSpec((tm, d // 2), lambda i, k=k: (k * nt + i, 0))
    return pl.pallas_call(
        _combine_kernel,
        out_shape=jax.ShapeDtypeStruct((n, d), F32),
        grid=(nt,),
        in_specs=[pl.BlockSpec((tm, d), lambda i: (i, 0))] + [slot(k) for k in range(TOP_EXPERTS)],
        out_specs=pl.BlockSpec((tm, d), lambda i: (i, 0)),
        compiler_params=_params("arbitrary"),
        name="moe_combine",
    )(x1, y_rows, y_rows, y_rows, y_rows)


def kernel(x, positions, g_attn, w_in, g_qa, g_ka, g_idx_k, lambda_q1, lambda_k1, lambda_q2, lambda_k2, g_qb, g_kb, g_subln, w_o, g_ffn, w_router, b_router, w_gu, b_gu, w_down, b_down):
    b, s, d = x.shape
    n = b * s
    k_sel = min(TOPK_KEYS, s // 4)
    c_tab, s_tab = _rope_tables(positions)
    for layer in range(g_attn.shape[0]):
        lam_init = 0.8 - 0.6 * math.exp(-0.3 * layer)
        x2 = x.reshape(n, d)
        aq, iq, bq, bk, bv, kka, kki, vv, iw = _in_proj(
            x2, g_attn[layer].reshape(1, d), w_in[layer], c_tab, s_tab,
            g_qa[layer], g_ka[layer], g_idx_k[layer], g_qb[layer], g_kb[layer])
        seq = lambda t: t.reshape(b, s, t.shape[-1])
        iw_t = jnp.transpose(seq(iw)[:, :, :IDX_HEADS], (0, 2, 1))
        y_a = _dsa_attn(seq(aq), seq(iq), seq(kka), seq(kki), seq(vv), iw_t, k_sel)
        lam = (jnp.exp(jnp.sum(lambda_q1[layer].astype(F32) * lambda_k1[layer].astype(F32)))
               - jnp.exp(jnp.sum(lambda_q2[layer].astype(F32) * lambda_k2[layer].astype(F32)))
               + lam_init).reshape(1)
        y_b = _diff_attn(seq(bq), seq(bk), seq(bv), lam, g_subln[layer].reshape(1, B_V_DIM), lam_init)
        x1, hp, top_idx, gates = _out_proj(
            x2, y_a.reshape(n, -1), y_b.reshape(n, -1), w_o[layer], g_ffn[layer].reshape(1, d),
            w_router[layer], b_router[layer])
        y_rows = _moe_experts(hp, top_idx[:, :TOP_EXPERTS], gates[:, :TOP_EXPERTS],
                              w_gu[layer], b_gu[layer], w_down[layer], b_down[layer])
        x = _moe_combine(x1, y_rows).reshape(b, s, d)
    return x
```

```python
import functools
import math

import jax
import jax.numpy as jnp
import numpy as np
from jax import lax
from jax.experimental import pallas as pl
from jax.experimental.pallas import tpu as pltpu

CHUNK = 64
ROPE_THETA = 500000.0
NORM_EPS = 1e-6
A_HEADS = 8
HEAD_DIM = 64
IDX_HEADS = 8
TOPK_KEYS = 256
B_HEADS = 4
B_V_DIM = 128
ROT_DIM = HEAD_DIM // 4
ROT_HALF = ROT_DIM // 2
N_EXPERTS = 32
TOP_EXPERTS = 4
SWIGLU_LIMIT = 7.0
SWIGLU_ALPHA = 1.702

LANES = 128
VMEM_LIMIT = 48 * 1024 * 1024

PROJ_ROWS = 512
DIFF_Q = 256
DIFF_K = 512
DSA_Q = 128
DSA_K = 256
MOE_ROWS = 256
COMBINE_ROWS = 256
ISSUE_GROUP = 4

MASKED = -1e30
INT_MIN = -(2 ** 31)

BF16 = jnp.bfloat16
F32 = jnp.float32
NT_DIMS = (((1,), (1,)), ((), ()))


def _params(*sem):
    return pltpu.CompilerParams(dimension_semantics=sem, vmem_limit_bytes=VMEM_LIMIT)


def _rope_kernel(invf_ref, pos_ref, cos_ref, sin_ref):
    f = pl.program_id(0)
    ang = pos_ref[...].astype(F32) * invf_ref[f]
    cos_ref[0] = jnp.cos(ang)
    sin_ref[0] = jnp.sin(ang)


def _rope_tables(positions):
    b, s = positions.shape
    inv_freq = ROPE_THETA ** (-jnp.arange(0, ROT_DIM, 2, dtype=F32) / ROT_DIM)
    cos_t, sin_t = pl.pallas_call(
        _rope_kernel,
        out_shape=(jax.ShapeDtypeStruct((ROT_HALF, b, s), F32),) * 2,
        grid=(ROT_HALF,),
        in_specs=[pl.BlockSpec(memory_space=pltpu.SMEM),
                  pl.BlockSpec((b, s), lambda f: (0, 0))],
        out_specs=(pl.BlockSpec((1, b, s), lambda f: (f, 0, 0)),) * 2,
        compiler_params=_params("arbitrary"),
        name="rope_tables",
    )(inv_freq, positions)
    n = b * s
    cos8 = jnp.transpose(cos_t, (1, 2, 0)).reshape(n, ROT_HALF)
    sin8 = jnp.transpose(sin_t, (1, 2, 0)).reshape(n, ROT_HALF)
    ones = jnp.ones((n, HEAD_DIM - ROT_DIM), F32)
    zeros = jnp.zeros((n, HEAD_DIM - ROT_DIM), F32)
    c_tab = jnp.concatenate([cos8, cos8, ones], axis=1)
    s_tab = jnp.concatenate([-sin8, sin8, zeros], axis=1)
    return jnp.tile(c_tab, (1, 2)), jnp.tile(s_tab, (1, 2))


def _segment_rms(y, gain):
    lane = lax.broadcasted_iota(jnp.int32, (1, LANES), 1)
    lo = lane < HEAD_DIM
    y2 = y * y
    s0 = jnp.sum(jnp.where(lo, y2, 0.0), axis=-1, keepdims=True)
    s1 = jnp.sum(jnp.where(lo, 0.0, y2), axis=-1, keepdims=True)
    ms = jnp.where(lo, s0, s1) * (1.0 / HEAD_DIM)
    return y * lax.rsqrt(ms + NORM_EPS) * gain


def _rope128(y, c_tab, s_tab):
    lane = lax.broadcasted_iota(jnp.int32, (1, LANES), 1)
    first = (lane % HEAD_DIM) < ROT_HALF
    up = pltpu.roll(y, LANES - ROT_HALF, 1)
    down = pltpu.roll(y, ROT_HALF, 1)
    return y * c_tab + jnp.where(first, up, down) * s_tab


def _in_proj_kernel(x_ref, g_ref, w_ref, c_ref, s_ref, gains_ref,
                    aq_ref, iq_ref, bq_ref, bk_ref, bv_ref, kka_ref, kki_ref, vt_ref, iw_ref):
    x = x_ref[...]
    ms = jnp.mean(x * x, axis=-1, keepdims=True)
    h = (x * lax.rsqrt(ms + NORM_EPS) * g_ref[...]).astype(BF16)
    c_tab = c_ref[...]
    s_tab = s_ref[...]
    wide = 4 * LANES

    def group(col, out_ref, gain_row, rope):
        acc = jnp.dot(h, w_ref[:, col:col + wide], preferred_element_type=F32)
        for t in range(4):
            y = acc[:, t * LANES:(t + 1) * LANES]
            if gain_row is not None:
                y = _segment_rms(y, gains_ref[gain_row:gain_row + 1, :])
            if rope:
                y = _rope128(y, c_tab, s_tab)
            out_ref[:, t * LANES:(t + 1) * LANES] = y.astype(out_ref.dtype)

    group(0 * wide, aq_ref, 0, True)
    group(1 * wide, iq_ref, None, True)
    group(2 * wide, bq_ref, 1, True)
    group(3 * wide, bk_ref, 2, True)
    group(4 * wide, bv_ref, None, False)
    acc = jnp.dot(h, w_ref[:, 5 * wide:6 * wide], preferred_element_type=F32)
    ka = _rope128(_segment_rms(acc[:, 0:LANES], gains_ref[3:4, :]), c_tab, s_tab)
    ki = _rope128(_segment_rms(acc[:, LANES:2 * LANES], gains_ref[4:5, :]), c_tab, s_tab)
    kka_ref[...] = ka.astype(BF16)
    kki_ref[...] = ki.astype(BF16)
    tk = vt_ref.shape[2]
    for c in range(vt_ref.shape[0]):
        v_t = acc[c * tk:(c + 1) * tk, 2 * LANES:3 * LANES].T
        vt_ref[c] = v_t[:HEAD_DIM].astype(BF16)
    iw_t = (acc[:, 3 * LANES:4 * LANES] * ((IDX_HEADS * HEAD_DIM) ** -0.5)).T
    iw_ref[...] = iw_t[:IDX_HEADS]


def _in_proj(x2, g_attn, w_in, c_tab, s_tab, g_qa, g_ka, g_idx_k, g_qb, g_kb):
    n, d = x2.shape
    ha = A_HEADS * HEAD_DIM
    o = np.cumsum([0, ha, HEAD_DIM, HEAD_DIM, IDX_HEADS * HEAD_DIM, HEAD_DIM, IDX_HEADS,
                   B_HEADS * 2 * HEAD_DIM, B_HEADS * 2 * HEAD_DIM, B_HEADS * B_V_DIM])
    aq, ak, av, iq, ik, iw, bq, bk, bv = [w_in[:, o[i]:o[i + 1]] for i in range(9)]
    pad = jnp.zeros((d, LANES - IDX_HEADS), w_in.dtype)
    w = jnp.concatenate([aq, iq, bq, bk, bv, ak, ak, ik, ik, av, av, iw, pad], axis=1).astype(BF16)
    two = lambda g: jnp.tile(g.reshape(1, HEAD_DIM), (1, 2))
    gains = jnp.concatenate([two(g_qa), two(g_qb), two(g_kb), two(g_ka), two(g_idx_k),
                             jnp.zeros((3, LANES), F32)], axis=0)
    tm = min(PROJ_ROWS, n)
    tk = min(DSA_K, tm)
    wide = 4 * LANES
    row = lambda width: pl.BlockSpec((tm, width), lambda i: (i, 0))
    full = lambda a: pl.BlockSpec(a.shape, lambda i: (0, 0))
    outs = pl.pallas_call(
        _in_proj_kernel,
        out_shape=(jax.ShapeDtypeStruct((n, wide), BF16),) * 5
        + (jax.ShapeDtypeStruct((n, LANES), BF16),) * 2
        + (jax.ShapeDtypeStruct((n // tk, HEAD_DIM, tk), BF16),
           jax.ShapeDtypeStruct((IDX_HEADS, n), F32)),
        grid=(n // tm,),
        in_specs=[row(d), full(g_attn), full(w), row(LANES), row(LANES), full(gains)],
        out_specs=(row(wide),) * 5 + (row(LANES),) * 2
        + (pl.BlockSpec((tm // tk, HEAD_DIM, tk), lambda i: (i, 0, 0)),
           pl.BlockSpec((IDX_HEADS, tm), lambda i: (0, i))),
        compiler_params=_params("arbitrary"),
        name="in_proj",
    )(x2, g_attn, w, c_tab, s_tab, gains)
    return outs


def _diff_attn_kernel(lam_ref, q_ref, k_ref, v_ref, g_ref, o_ref, *, lam_init):
    i = pl.program_id(1)
    tq = q_ref.shape[1]
    tk = DIFF_K
    lam = lam_ref[0]
    nkb = (i * tq + tq + tk - 1) // tk
    scale = HEAD_DIM ** -0.5
    lane = lax.broadcasted_iota(jnp.int32, (1, LANES), 1)
    lo = lane < HEAD_DIM
    q_chunk = (i * tq + lax.broadcasted_iota(jnp.int32, (2 * tq, 1), 0) % tq) // CHUNK
    k_chunk0 = lax.broadcasted_iota(jnp.int32, (1, tk), 1) // CHUNK

    for hd in range(B_HEADS):
        cols = slice(hd * LANES, (hd + 1) * LANES)
        qh = q_ref[0, :, cols]
        zero = jnp.zeros_like(qh)
        qs = jnp.concatenate([jnp.where(lo, qh, zero), jnp.where(lo, zero, qh)], axis=0)

        def body(j, carry):
            m, l, acc = carry
            start = pl.multiple_of(j * tk, tk)
            kj = k_ref[0, pl.ds(start, tk), cols]
            vj = v_ref[0, pl.ds(start, tk), cols]
            s = lax.dot_general(qs, kj, NT_DIMS, preferred_element_type=F32) * scale
            s = jnp.where(k_chunk0 + j * (tk // CHUNK) <= q_chunk, s, MASKED)
            m_new = jnp.maximum(m, jnp.max(s, axis=-1, keepdims=True))
            alpha = jnp.exp(m - m_new)
            p = jnp.exp(s - m_new)
            l = alpha * l + jnp.sum(p, axis=-1, keepdims=True)
            pv = jnp.concatenate(
                [jnp.dot(p[:tq].astype(BF16), vj, preferred_element_type=F32),
                 jnp.dot(p[tq:].astype(BF16), vj, preferred_element_type=F32)], axis=0)
            return m_new, l, alpha * acc + pv

        init = (jnp.full((2 * tq, 1), MASKED, F32), jnp.zeros((2 * tq, 1), F32),
                jnp.zeros((2 * tq, LANES), F32))
        _, l, acc = lax.fori_loop(0, nkb, body, init)
        o = acc / l
        o = o[:tq] - lam * o[tq:]
        ms = jnp.mean(o * o, axis=-1, keepdims=True)
        o = o * lax.rsqrt(ms + NORM_EPS) * g_ref[...] * (1.0 - lam_init)
        o_ref[0, :, cols] = o.astype(o_ref.dtype)


def _diff_attn(bq, bk, bv, lam, g_subln, lam_init):
    b, s, w = bq.shape
    tq = min(DIFF_Q, s)
    kv = pl.BlockSpec((1, s, w), lambda bi, i: (bi, 0, 0))
    qo = pl.BlockSpec((1, tq, w), lambda bi, i: (bi, i, 0))
    return pl.pallas_call(
        functools.partial(_diff_attn_kernel, lam_init=lam_init),
        out_shape=jax.ShapeDtypeStruct((b, s, w), BF16),
        grid=(b, s // tq),
        in_specs=[pl.BlockSpec(memory_space=pltpu.SMEM), qo, kv, kv,
                  pl.BlockSpec((1, B_V_DIM), lambda bi, i: (0, 0))],
        out_specs=qo,
        compiler_params=_params("arbitrary", "arbitrary"),
        name="diff_attn",
    )(lam, bq, bk, bv, g_subln)


def _stack_heads(src_ref, dst_ref, scale):
    tq = src_ref.shape[1]
    lane = lax.broadcasted_iota(jnp.int32, (1, LANES), 1)
    lo = lane < HEAD_DIM
    for hd in range(A_HEADS):
        blk = src_ref[0, :, (hd // 2) * LANES:(hd // 2 + 1) * LANES]
        if scale is not None:
            blk = blk * jnp.asarray(scale, blk.dtype)
        keep = lo if hd % 2 == 0 else jnp.logical_not(lo)
        dst_ref[hd * tq:(hd + 1) * tq, :] = jnp.where(keep, blk, jnp.zeros_like(blk))


def _dsa_attn_kernel(aq_ref, iq_ref, kka_ref, kki_ref, vt_ref, iw_ref, o_ref,
                     qa_ref, qi_ref, key_ref, *, k_sel):
    i = pl.program_id(1)
    tq = aq_ref.shape[1]
    tk = key_ref.shape[1]
    nsel = (i * tq + tq + tk - 1) // tk
    _stack_heads(aq_ref, qa_ref, HEAD_DIM ** -0.5)
    _stack_heads(iq_ref, qi_ref, None)
    sub = lax.broadcasted_iota(jnp.int32, (tk, tq), 0)
    k_chunk0 = sub // CHUNK
    q_chunk = (i * tq + lax.broadcasted_iota(jnp.int32, (tk, tq), 1)) // CHUNK

    def score_block(j, _):
        start = pl.multiple_of(j * tk, tk)
        kj = kki_ref[0, pl.ds(start, tk), :]
        st = lax.dot_general(kj, qi_ref[...], NT_DIMS, preferred_element_type=F32)
        score = jnp.zeros((tk, tq), F32)
        for hd in range(IDX_HEADS):
            rel = jnp.maximum(st[:, hd * tq:(hd + 1) * tq], 0.0)
            score = score + rel * iw_ref[hd:hd + 1, :]
        bits = pltpu.bitcast(score, jnp.int32)
        key = jnp.where(bits < 0, jnp.int32(INT_MIN) - bits, bits)
        key_ref[j] = jnp.where(k_chunk0 + j * (tk // CHUNK) > q_chunk, jnp.int32(INT_MIN), key)
        return 0

    lax.fori_loop(0, nsel, score_block, 0)

    def count(pred):
        def body(j, acc):
            hit = jnp.where(pred(key_ref[j], j), 1, 0).astype(jnp.int32)
            return acc + jnp.sum(hit.reshape(tk // 8, 8, tq), axis=0)
        acc = lax.fori_loop(0, nsel, body, jnp.zeros((8, tq), jnp.int32))
        return jnp.sum(acc, axis=0, keepdims=True)

    def search_bit(t, thr):
        cand = thr + lax.shift_left(jnp.int32(1), 31 - t)
        cnt = count(lambda key, j: key >= cand)
        return jnp.where(cnt >= k_sel, cand, thr)

    thr = lax.fori_loop(0, 32, search_bit, jnp.full((1, tq), INT_MIN, jnp.int32))
    n_gt = count(lambda key, j: key > thr)
    n_eq = count(lambda key, j: key == thr)
    need = k_sel - n_gt
    tie_break = jnp.logical_and(thr > INT_MIN, n_eq > need)

    def tie_search(_):
        def bit(t, cut):
            cand = cut + lax.shift_left(jnp.int32(1), 15 - t)
            below = count(lambda key, j: jnp.logical_and(key == thr, sub + j * tk < cand))
            return jnp.where(below < need, cand, cut)
        return lax.fori_loop(0, 16, bit, jnp.zeros((1, tq), jnp.int32))

    any_tie = jnp.max(jnp.where(tie_break, 1, 0)) > 0
    cut = lax.cond(any_tie, tie_search, lambda _: jnp.full((1, tq), 2 ** 30, jnp.int32), 0)
    cut = jnp.where(tie_break, cut, 2 ** 30)

    def attn_block(j, carry):
        m, l, acc = carry
        start = pl.multiple_of(j * tk, tk)
        kj = kka_ref[0, pl.ds(start, tk), :]
        st = lax.dot_general(kj, qa_ref[...], NT_DIMS, preferred_element_type=F32)
        key = key_ref[j]
        sel = jnp.logical_or(key > thr, jnp.logical_and(key == thr, sub + j * tk <= cut))
        sel = jnp.logical_and(sel, key > INT_MIN)
        s = jnp.concatenate([jnp.where(sel, st[:, hd * tq:(hd + 1) * tq], MASKED)
                             for hd in range(A_HEADS)], axis=1)
        m_new = jnp.maximum(m, jnp.max(s, axis=0, keepdims=True))
        alpha = jnp.exp(m - m_new)
        p = jnp.exp(s - m_new)
        l = alpha * l + jnp.sum(p, axis=0, keepdims=True)
        acc = alpha * acc + jnp.dot(vt_ref[j], p.astype(BF16), preferred_element_type=F32)
        return m_new, l, acc

    rows = A_HEADS * tq
    init = (jnp.full((1, rows), MASKED, F32), jnp.zeros((1, rows), F32), jnp.zeros((HEAD_DIM, rows), F32))
    _, l, acc = lax.fori_loop(0, nsel, attn_block, init)
    out_t = acc / l
    for g in range(A_HEADS // 2):
        pair = out_t[:, 2 * g * tq:(2 * g + 2) * tq]
        pair = jnp.concatenate([pair[:, :tq], pair[:, tq:]], axis=0)
        o_ref[0, :, g * LANES:(g + 1) * LANES] = pair.T.astype(o_ref.dtype)


def _dsa_attn(aq, iq, kka, kki, v_t, iw_t, k_sel):
    b, s, w = aq.shape
    tq = min(DSA_Q, s)
    tk = v_t.shape[2]
    nq = s // tq
    qo = pl.BlockSpec((1, tq, w), lambda bi, i: (bi, i, 0))
    kv = pl.BlockSpec((1, s, LANES), lambda bi, i: (bi, 0, 0))
    return pl.pallas_call(
        functools.partial(_dsa_attn_kernel, k_sel=k_sel),
        out_shape=jax.ShapeDtypeStruct((b, s, w), BF16),
        grid=(b, nq),
        in_specs=[qo, qo, kv, kv,
                  pl.BlockSpec((s // tk, HEAD_DIM, tk), lambda bi, i: (bi, 0, 0)),
                  pl.BlockSpec((IDX_HEADS, tq), lambda bi, i: (0, bi * nq + i))],
        out_specs=qo,
        scratch_shapes=[pltpu.VMEM((A_HEADS * tq, LANES), BF16),
                        pltpu.VMEM((IDX_HEADS * tq, LANES), BF16),
                        pltpu.VMEM((s // tk, tk, tq), jnp.int32)],
        compiler_params=_params("arbitrary", "arbitrary"),
        name="dsa_attn",
    )(aq, iq, kka, kki, v_t, iw_t)


def _pack_halves(y):
    w = y.shape[1] // 2
    r = y.astype(BF16).astype(F32)
    lo = pltpu.bitcast(r[:, :w], jnp.uint32)
    hi = pltpu.bitcast(r[:, w:], jnp.uint32)
    return jnp.bitwise_or(jnp.bitwise_and(hi, jnp.uint32(0xFFFF0000)), lax.shift_right_logical(lo, jnp.uint32(16)))


def _unpack_halves(u):
    lo = pltpu.bitcast(lax.shift_left(u, jnp.uint32(16)), F32)
    hi = pltpu.bitcast(jnp.bitwise_and(u, jnp.uint32(0xFFFF0000)), F32)
    return lo, hi


SLAB = 4


def _store_slab(ref, first, packed):
    rows = packed.shape[0]
    for c in range(SLAB):
        ref[pl.ds(SLAB * first + c, rows, stride=SLAB), :] = packed[:, c * LANES:(c + 1) * LANES]


def _load_slab(ref, first, rows):
    return jnp.concatenate([ref[pl.ds(SLAB * first + c, rows, stride=SLAB), :] for c in range(SLAB)], axis=1)


def _out_proj_kernel(x_ref, ya_ref, yb_ref, wa_ref, wb_ref, g_ref, wr_ref, br_ref,
                     x1_ref, hp_ref, idx_ref, gate_ref, cnt_ref):
    x1 = (x_ref[...] + jnp.dot(ya_ref[...], wa_ref[...], preferred_element_type=F32)
          + jnp.dot(yb_ref[...], wb_ref[...], preferred_element_type=F32))
    x1_ref[...] = x1
    ms = jnp.mean(x1 * x1, axis=-1, keepdims=True)
    h = x1 * lax.rsqrt(ms + NORM_EPS) * g_ref[...]
    _store_slab(hp_ref, 0, _pack_halves(h))
    logits = jnp.dot(h.astype(BF16), wr_ref[...], preferred_element_type=F32) + br_ref[...]
    tm = logits.shape[0]
    lane = lax.broadcasted_iota(jnp.int32, logits.shape, 1)
    work = jnp.where(lane < N_EXPERTS, logits, -jnp.inf)
    val_out = jnp.full(logits.shape, -jnp.inf, F32)
    member = jnp.zeros(logits.shape, F32)
    args = []
    for r in range(TOP_EXPERTS):
        top = jnp.max(work, axis=-1, keepdims=True)
        arg = jnp.min(jnp.where(work == top, lane, LANES), axis=-1, keepdims=True)
        args.append(arg)
        val_out = jnp.where(lane == r, top, val_out)
        member = jnp.where(lane == arg, 1.0, member)
        work = jnp.where(lane == arg, -jnp.inf, work)
    e = jnp.exp(val_out - jnp.max(val_out, axis=-1, keepdims=True))
    gate_ref[...] = e / jnp.sum(e, axis=-1, keepdims=True)
    earlier = jnp.where(lax.broadcasted_iota(jnp.int32, (tm, tm), 1)
                        < lax.broadcasted_iota(jnp.int32, (tm, tm), 0), 1.0, 0.0).astype(BF16)
    before = jnp.dot(earlier, member.astype(BF16), preferred_element_type=F32)
    idx_out = jnp.zeros(logits.shape, jnp.int32)
    for r in range(TOP_EXPERTS):
        rank = jnp.sum(jnp.where(lane == args[r], before, 0.0), axis=-1, keepdims=True)
        idx_out = jnp.where(lane == r, args[r], idx_out)
        idx_out = jnp.where(lane == TOP_EXPERTS + r, rank.astype(jnp.int32), idx_out)
    idx_ref[...] = idx_out
    counts = jnp.sum(member, axis=0, keepdims=True).astype(jnp.int32)
    cnt_ref[0] = jnp.broadcast_to(counts, (8, LANES))


def _out_proj(x2, y_a, y_b, w_o, g_ffn, w_router, b_router):
    n, d = x2.shape
    da = y_a.shape[1]
    wa = w_o[:da].astype(BF16)
    wb = w_o[da:].astype(BF16)
    wr = jnp.concatenate([w_router, jnp.zeros((d, LANES - N_EXPERTS), w_router.dtype)], 1).astype(BF16)
    br = jnp.concatenate([b_router, jnp.zeros((LANES - N_EXPERTS,), F32)]).reshape(1, LANES)
    tm = min(PROJ_ROWS, n)
    row = lambda width: pl.BlockSpec((tm, width), lambda i: (i, 0))
    full = lambda a: pl.BlockSpec(a.shape, lambda i: (0, 0))
    return pl.pallas_call(
        _out_proj_kernel,
        out_shape=(jax.ShapeDtypeStruct((n, d), F32), jax.ShapeDtypeStruct((SLAB * n, LANES), jnp.uint32),
                   jax.ShapeDtypeStruct((n, LANES), jnp.int32), jax.ShapeDtypeStruct((n, LANES), F32),
                   jax.ShapeDtypeStruct((n // tm, 8, LANES), jnp.int32)),
        grid=(n // tm,),
        in_specs=[row(d), row(da), row(y_b.shape[1]), full(wa), full(wb), full(g_ffn), full(wr), full(br)],
        out_specs=(row(d), pl.BlockSpec((SLAB * tm, LANES), lambda i: (i, 0)), row(LANES), row(LANES),
                   pl.BlockSpec((1, 8, LANES), lambda i: (i, 0, 0))),
        compiler_params=_params("arbitrary"),
        name="out_proj",
    )(x2, y_a, y_b, wa, wb, g_ffn, wr, br)


def _moe_plan(top_idx, counts, tile_rows):
    n = top_idx.shape[0]
    rows = MOE_ROWS
    n_blocks = -(-(n * TOP_EXPERTS + N_EXPERTS * (rows - 1)) // rows)
    total = jnp.sum(counts, axis=0)
    padded = ((total + rows - 1) // rows) * rows
    pend = jnp.cumsum(padded)
    base = (pend - padded)[None, :] + jnp.cumsum(counts, axis=0) - counts
    expert = top_idx[:, :TOP_EXPERTS]
    rank = top_idx[:, TOP_EXPERTS:2 * TOP_EXPERTS]
    base_tok = jnp.repeat(base, tile_rows, axis=0)
    hit = expert[:, :, None] == jnp.arange(N_EXPERTS, dtype=jnp.int32)[None, None, :]
    dest = jnp.sum(jnp.where(hit, base_tok[:, None, :], 0), axis=-1) + rank
    block_start = jnp.arange(n_blocks, dtype=jnp.int32) * rows
    block_expert = jnp.minimum(jnp.sum((block_start[:, None] >= pend[None, :]).astype(jnp.int32), axis=1),
                               N_EXPERTS - 1).astype(jnp.int32)
    n_used = (pend[-1] // rows).astype(jnp.int32).reshape(1)
    return dest.astype(jnp.int32).reshape(-1), block_expert, n_used, n_blocks


def _dispatch_kernel(dest_ref, h_hbm, xs_in, xs_hbm, sem):
    del xs_in
    i = pl.program_id(0)
    nsteps = pl.num_programs(0)
    per_step = dest_ref.shape[2]
    tm = per_step // TOP_EXPERTS

    def issue(g, _):
        first = g * (ISSUE_GROUP * TOP_EXPERTS)
        slots = [dest_ref[0, 0, first + j] for j in range(ISSUE_GROUP * TOP_EXPERTS)]
        for j, slot in enumerate(slots):
            t = g * ISSUE_GROUP + j // TOP_EXPERTS
            src = h_hbm.at[pl.ds(pl.multiple_of(SLAB * (i * tm + t), SLAB), SLAB)]
            dst = xs_hbm.at[pl.ds(pl.multiple_of(SLAB * slot, SLAB), SLAB)]
            pltpu.make_async_copy(src, dst, sem).start(priority=j % 2)
        return 0

    lax.fori_loop(0, tm // ISSUE_GROUP, issue, 0)

    def wait_one_step():
        whole = xs_hbm.at[pl.ds(0, SLAB * per_step)]
        pltpu.make_async_copy(whole, whole, sem).wait()

    @pl.when(i > 0)
    def _():
        wait_one_step()

    @pl.when(i == nsteps - 1)
    def _():
        wait_one_step()


def _moe_dispatch(hp, dest, m):
    n = hp.shape[0] // SLAB
    tm = min(PROJ_ROWS, n)
    per_step = tm * TOP_EXPERTS
    return pl.pallas_call(
        _dispatch_kernel,
        out_shape=jax.ShapeDtypeStruct((SLAB * m, LANES), jnp.uint32),
        grid=(n // tm,),
        in_specs=[pl.BlockSpec((1, 1, per_step), lambda i: (i, 0, 0), memory_space=pltpu.SMEM),
                  pl.BlockSpec(memory_space=pl.ANY), pl.BlockSpec(memory_space=pl.ANY)],
        out_specs=pl.BlockSpec(memory_space=pl.ANY),
        scratch_shapes=[pltpu.SemaphoreType.DMA],
        input_output_aliases={2: 0},
        compiler_params=_params("arbitrary"),
        name="moe_dispatch",
    )(dest.reshape(n // tm, 1, per_step), hp, jnp.zeros((SLAB * m, LANES), jnp.uint32))


def _moe_kernel(bexp_ref, nused_ref, x_ref, wgu_ref, bgu_ref, wd_ref, bd_ref, y_ref):
    del bexp_ref
    rows = x_ref.shape[0] // SLAB
    half = SLAB * LANES
    f = wd_ref.shape[1]

    @pl.when(pl.program_id(0) < nused_ref[0])
    def _():
        lo, hi = _unpack_halves(_load_slab(x_ref, 0, rows))
        gu = (jnp.dot(lo.astype(BF16), wgu_ref[0, :half, :], preferred_element_type=F32)
              + jnp.dot(hi.astype(BF16), wgu_ref[0, half:, :], preferred_element_type=F32)
              + bgu_ref[0])
        glu = jnp.minimum(gu[:, :f], SWIGLU_LIMIT)
        lin = jnp.clip(gu[:, f:], -SWIGLU_LIMIT, SWIGLU_LIMIT)
        act = glu * jax.nn.sigmoid(SWIGLU_ALPHA * glu) * (lin + 1.0)
        y = jnp.dot(act.astype(BF16), wd_ref[0], preferred_element_type=F32) + bd_ref[0]
        _store_slab(y_ref, 0, _pack_halves(y))

    @pl.when(pl.program_id(0) >= nused_ref[0])
    def _():
        y_ref[...] = jnp.zeros(y_ref.shape, y_ref.dtype)


def _moe_experts(xs, block_expert, n_used, w_gu, b_gu, w_down, b_down):
    m = xs.shape[0] // SLAB
    d = 2 * SLAB * LANES
    f = w_down.shape[1]
    rows = MOE_ROWS
    wgu = w_gu.astype(BF16)
    wd = w_down.astype(BF16)
    bgu = b_gu.reshape(N_EXPERTS, 1, 2 * f)
    bd = b_down.reshape(N_EXPERTS, 1, d)
    slots = pl.BlockSpec((SLAB * rows, LANES), lambda i, be, nu: (i, 0))
    by_expert = lambda shape: pl.BlockSpec((1,) + shape, lambda i, be, nu: (be[i], 0, 0))
    return pl.pallas_call(
        _moe_kernel,
        out_shape=jax.ShapeDtypeStruct((SLAB * m, LANES), jnp.uint32),
        grid_spec=pltpu.PrefetchScalarGridSpec(
            num_scalar_prefetch=2,
            grid=(m // rows,),
            in_specs=[slots, by_expert((d, 2 * f)), by_expert((1, 2 * f)),
                      by_expert((f, d)), by_expert((1, d))],
            out_specs=slots),
        compiler_params=_params("arbitrary"),
        name="moe_experts",
    )(block_expert, n_used, xs, wgu, bgu, wd, bd)


def _combine_kernel(dcur_ref, dnext_ref, y_hbm, x_ref, gate_ref, o_ref, ybuf, sems):
    i = pl.program_id(0)
    nsteps = pl.num_programs(0)
    tm = x_ref.shape[0]
    half = SLAB * LANES
    slot = lax.rem(i, 2)

    def issue(dest_ref, buf):
        def body(g, _):
            first = g * (ISSUE_GROUP * TOP_EXPERTS)
            slots = [dest_ref[0, 0, first + j] for j in range(ISSUE_GROUP * TOP_EXPERTS)]
            for j, slot in enumerate(slots):
                t = g * ISSUE_GROUP + j // TOP_EXPERTS
                src = pl.multiple_of(SLAB * slot, SLAB)
                dst = pl.multiple_of(SLAB * ((j % TOP_EXPERTS) * tm + t), SLAB)
                pltpu.make_async_copy(y_hbm.at[pl.ds(src, SLAB)], ybuf.at[buf, pl.ds(dst, SLAB)],
                                      sems.at[buf]).start(priority=j % 2)
            return 0
        lax.fori_loop(0, tm // ISSUE_GROUP, body, 0)

    @pl.when(i == 0)
    def _():
        issue(dcur_ref, 0)

    @pl.when(i + 1 < nsteps)
    def _():
        issue(dnext_ref, 1 - slot)

    pltpu.make_async_copy(y_hbm.at[pl.ds(0, SLAB * TOP_EXPERTS * tm)], ybuf.at[slot], sems.at[slot]).wait()
    lo = x_ref[:, :half]
    hi = x_ref[:, half:]
    for k in range(TOP_EXPERTS):
        a, b = _unpack_halves(_load_slab(ybuf.at[slot], k * tm, tm))
        g = gate_ref[:, k:k + 1]
        lo = lo + a * g
        hi = hi + b * g
    o_ref[:, :half] = lo
    o_ref[:, half:] = hi


def _moe_combine(x1, y_rows, dest, gates):
    n, d = x1.shape
    tm = min(COMBINE_ROWS, n)
    nt = n // tm
    per_step = tm * TOP_EXPERTS
    dest3 = dest.reshape(nt, 1, per_step)
    return pl.pallas_call(
        _combine_kernel,
        out_shape=jax.ShapeDtypeStruct((n, d), F32),
        grid=(nt,),
        in_specs=[pl.BlockSpec((1, 1, per_step), lambda i: (i, 0, 0), memory_space=pltpu.SMEM),
                  pl.BlockSpec((1, 1, per_step), lambda i: (jnp.minimum(i + 1, nt - 1), 0, 0),
                               memory_space=pltpu.SMEM),
                  pl.BlockSpec(memory_space=pl.ANY),
                  pl.BlockSpec((tm, d), lambda i: (i, 0)),
                  pl.BlockSpec((tm, LANES), lambda i: (i, 0))],
        out_specs=pl.BlockSpec((tm, d), lambda i: (i, 0)),
        scratch_shapes=[pltpu.VMEM((2, SLAB * per_step, LANES), jnp.uint32), pltpu.SemaphoreType.DMA((2,))],
        compiler_params=_params("arbitrary"),
        name="moe_combine",
    )(dest3, dest3, y_rows, x1, gates)


def kernel(x, positions, g_attn, w_in, g_qa, g_ka, g_idx_k, lambda_q1, lambda_k1, lambda_q2, lambda_k2, g_qb, g_kb, g_subln, w_o, g_ffn, w_router, b_router, w_gu, b_gu, w_down, b_down):
    b, s, d = x.shape
    n = b * s
    k_sel = min(TOPK_KEYS, s // 4)
    c_tab, s_tab = _rope_tables(positions)
    for layer in range(g_attn.shape[0]):
        lam_init = 0.8 - 0.6 * math.exp(-0.3 * layer)
        x2 = x.reshape(n, d)
        aq, iq, bq, bk, bv, kka, kki, v_t, iw_t = _in_proj(
            x2, g_attn[layer].reshape(1, d), w_in[layer], c_tab, s_tab,
            g_qa[layer], g_ka[layer], g_idx_k[layer], g_qb[layer], g_kb[layer])
        seq = lambda t: t.reshape(b, s, t.shape[-1])
        y_a = _dsa_attn(seq(aq), seq(iq), seq(kka), seq(kki), v_t, iw_t, k_sel)
        lam = (jnp.exp(jnp.sum(lambda_q1[layer].astype(F32) * lambda_k1[layer].astype(F32)))
               - jnp.exp(jnp.sum(lambda_q2[layer].astype(F32) * lambda_k2[layer].astype(F32)))
               + lam_init).reshape(1)
        y_b = _diff_attn(seq(bq), seq(bk), seq(bv), lam, g_subln[layer].reshape(1, B_V_DIM), lam_init)
        x1, hp, top_idx, gates, counts = _out_proj(
            x2, y_a.reshape(n, -1), y_b.reshape(n, -1), w_o[layer], g_ffn[layer].reshape(1, d),
            w_router[layer], b_router[layer])
        dest, block_expert, n_used, n_blocks = _moe_plan(
            top_idx, counts[:, 0, :N_EXPERTS], n // counts.shape[0])
        xs = _moe_dispatch(hp, dest, n_blocks * MOE_ROWS)
        y_rows = _moe_experts(xs, block_expert, n_used, w_gu[layer], b_gu[layer], w_down[layer], b_down[layer])
        x = _moe_combine(x1, y_rows, dest, gates).reshape(b, s, d)
    return x
```

```python
import functools
import math

import jax
import jax.numpy as jnp
import numpy as np
from jax import lax
from jax.experimental import pallas as pl
from jax.experimental.pallas import tpu as pltpu

CHUNK = 64
ROPE_THETA = 500000.0
NORM_EPS = 1e-6
A_HEADS = 8
HEAD_DIM = 64
IDX_HEADS = 8
TOPK_KEYS = 256
B_HEADS = 4
B_V_DIM = 128
ROT_DIM = HEAD_DIM // 4
ROT_HALF = ROT_DIM // 2
N_EXPERTS = 32
TOP_EXPERTS = 4
SWIGLU_LIMIT = 7.0
SWIGLU_ALPHA = 1.702

LANES = 128
VMEM_LIMIT = 48 * 1024 * 1024

PROJ_ROWS = 512
DIFF_Q = 256
DIFF_K = 512
DSA_Q = 128
DSA_K = 256
MOE_ROWS = 256
COMBINE_ROWS = 256
ISSUE_GROUP = 4

MASKED = -1e30
INT_MIN = -(2 ** 31)

BF16 = jnp.bfloat16
F32 = jnp.float32
NT_DIMS = (((1,), (1,)), ((), ()))


def _params(*sem):
    return pltpu.CompilerParams(dimension_semantics=sem, vmem_limit_bytes=VMEM_LIMIT)


def _rope_kernel(invf_ref, pos_ref, cos_ref, sin_ref):
    f = pl.program_id(0)
    ang = pos_ref[...].astype(F32) * invf_ref[f]
    cos_ref[0] = jnp.cos(ang)
    sin_ref[0] = jnp.sin(ang)


def _rope_tables(positions):
    b, s = positions.shape
    inv_freq = ROPE_THETA ** (-jnp.arange(0, ROT_DIM, 2, dtype=F32) / ROT_DIM)
    cos_t, sin_t = pl.pallas_call(
        _rope_kernel,
        out_shape=(jax.ShapeDtypeStruct((ROT_HALF, b, s), F32),) * 2,
        grid=(ROT_HALF,),
        in_specs=[pl.BlockSpec(memory_space=pltpu.SMEM),
                  pl.BlockSpec((b, s), lambda f: (0, 0))],
        out_specs=(pl.BlockSpec((1, b, s), lambda f: (f, 0, 0)),) * 2,
        compiler_params=_params("arbitrary"),
        name="rope_tables",
    )(inv_freq, positions)
    n = b * s
    cos8 = jnp.transpose(cos_t, (1, 2, 0)).reshape(n, ROT_HALF)
    sin8 = jnp.transpose(sin_t, (1, 2, 0)).reshape(n, ROT_HALF)
    ones = jnp.ones((n, HEAD_DIM - ROT_DIM), F32)
    zeros = jnp.zeros((n, HEAD_DIM - ROT_DIM), F32)
    c_tab = jnp.concatenate([cos8, cos8, ones], axis=1)
    s_tab = jnp.concatenate([-sin8, sin8, zeros], axis=1)
    return jnp.tile(c_tab, (1, 2)), jnp.tile(s_tab, (1, 2))


def _segment_rms(y, gain):
    lane = lax.broadcasted_iota(jnp.int32, (1, LANES), 1)
    lo = lane < HEAD_DIM
    y2 = y * y
    s0 = jnp.sum(jnp.where(lo, y2, 0.0), axis=-1, keepdims=True)
    s1 = jnp.sum(jnp.where(lo, 0.0, y2), axis=-1, keepdims=True)
    ms = jnp.where(lo, s0, s1) * (1.0 / HEAD_DIM)
    return y * lax.rsqrt(ms + NORM_EPS) * gain


def _rope128(y, c_tab, s_tab):
    lane = lax.broadcasted_iota(jnp.int32, (1, LANES), 1)
    first = (lane % HEAD_DIM) < ROT_HALF
    up = pltpu.roll(y, LANES - ROT_HALF, 1)
    down = pltpu.roll(y, ROT_HALF, 1)
    return y * c_tab + jnp.where(first, up, down) * s_tab


def _in_proj_kernel(x_ref, g_ref, w_ref, c_ref, s_ref, gains_ref,
                    aq_ref, iq_ref, bq_ref, bk_ref, bv_ref, kka_ref, kki_ref, vt_ref, iw_ref):
    x = x_ref[...]
    ms = jnp.mean(x * x, axis=-1, keepdims=True)
    h = (x * lax.rsqrt(ms + NORM_EPS) * g_ref[...]).astype(BF16)
    c_tab = c_ref[...]
    s_tab = s_ref[...]
    wide = 4 * LANES

    def group(col, out_ref, gain_row, rope):
        acc = jnp.dot(h, w_ref[:, col:col + wide], preferred_element_type=F32)
        for t in range(4):
            y = acc[:, t * LANES:(t + 1) * LANES]
            if gain_row is not None:
                y = _segment_rms(y, gains_ref[gain_row:gain_row + 1, :])
            if rope:
                y = _rope128(y, c_tab, s_tab)
            out_ref[:, t * LANES:(t + 1) * LANES] = y.astype(out_ref.dtype)

    group(0 * wide, aq_ref, 0, True)
    group(1 * wide, iq_ref, None, True)
    group(2 * wide, bq_ref, 1, True)
    group(3 * wide, bk_ref, 2, True)
    group(4 * wide, bv_ref, None, False)
    acc = jnp.dot(h, w_ref[:, 5 * wide:6 * wide], preferred_element_type=F32)
    ka = _rope128(_segment_rms(acc[:, 0:LANES], gains_ref[3:4, :]), c_tab, s_tab)
    ki = _rope128(_segment_rms(acc[:, LANES:2 * LANES], gains_ref[4:5, :]), c_tab, s_tab)
    kka_ref[...] = ka.astype(BF16)
    kki_ref[...] = ki.astype(BF16)
    tk = vt_ref.shape[2]
    for c in range(vt_ref.shape[0]):
        v_t = acc[c * tk:(c + 1) * tk, 2 * LANES:3 * LANES].T
        vt_ref[c] = v_t[:HEAD_DIM].astype(BF16)
    iw_t = (acc[:, 3 * LANES:4 * LANES] * ((IDX_HEADS * HEAD_DIM) ** -0.5)).T
    iw_ref[...] = iw_t[:IDX_HEADS]


def _in_proj(x2, g_attn, w_in, c_tab, s_tab, g_qa, g_ka, g_idx_k, g_qb, g_kb):
    n, d = x2.shape
    ha = A_HEADS * HEAD_DIM
    o = np.cumsum([0, ha, HEAD_DIM, HEAD_DIM, IDX_HEADS * HEAD_DIM, HEAD_DIM, IDX_HEADS,
                   B_HEADS * 2 * HEAD_DIM, B_HEADS * 2 * HEAD_DIM, B_HEADS * B_V_DIM])
    aq, ak, av, iq, ik, iw, bq, bk, bv = [w_in[:, o[i]:o[i + 1]] for i in range(9)]
    pad = jnp.zeros((d, LANES - IDX_HEADS), w_in.dtype)
    w = jnp.concatenate([aq, iq, bq, bk, bv, ak, ak, ik, ik, av, av, iw, pad], axis=1).astype(BF16)
    two = lambda g: jnp.tile(g.reshape(1, HEAD_DIM), (1, 2))
    gains = jnp.concatenate([two(g_qa), two(g_qb), two(g_kb), two(g_ka), two(g_idx_k),
                             jnp.zeros((3, LANES), F32)], axis=0)
    tm = min(PROJ_ROWS, n)
    tk = min(DSA_K, tm)
    wide = 4 * LANES
    row = lambda width: pl.BlockSpec((tm, width), lambda i: (i, 0))
    full = lambda a: pl.BlockSpec(a.shape, lambda i: (0, 0))
    outs = pl.pallas_call(
        _in_proj_kernel,
        out_shape=(jax.ShapeDtypeStruct((n, wide), BF16),) * 5
        + (jax.ShapeDtypeStruct((n, LANES), BF16),) * 2
        + (jax.ShapeDtypeStruct((n // tk, HEAD_DIM, tk), BF16),
           jax.ShapeDtypeStruct((IDX_HEADS, n), F32)),
        grid=(n // tm,),
        in_specs=[row(d), full(g_attn), full(w), row(LANES), row(LANES), full(gains)],
        out_specs=(row(wide),) * 5 + (row(LANES),) * 2
        + (pl.BlockSpec((tm // tk, HEAD_DIM, tk), lambda i: (i, 0, 0)),
           pl.BlockSpec((IDX_HEADS, tm), lambda i: (0, i))),
        compiler_params=_params("arbitrary"),
        name="in_proj",
    )(x2, g_attn, w, c_tab, s_tab, gains)
    return outs


def _diff_attn_kernel(lam_ref, q_ref, k_ref, v_ref, g_ref, o_ref, *, lam_init):
    i = pl.program_id(1)
    tq = q_ref.shape[1]
    tk = DIFF_K
    lam = lam_ref[0]
    nkb = (i * tq + tq + tk - 1) // tk
    scale = HEAD_DIM ** -0.5
    lane = lax.broadcasted_iota(jnp.int32, (1, LANES), 1)
    lo = lane < HEAD_DIM
    q_chunk = (i * tq + lax.broadcasted_iota(jnp.int32, (2 * tq, 1), 0) % tq) // CHUNK
    k_chunk0 = lax.broadcasted_iota(jnp.int32, (1, tk), 1) // CHUNK

    for hd in range(B_HEADS):
        cols = slice(hd * LANES, (hd + 1) * LANES)
        qh = q_ref[0, :, cols]
        zero = jnp.zeros_like(qh)
        qs = jnp.concatenate([jnp.where(lo, qh, zero), jnp.where(lo, zero, qh)], axis=0)

        def body(j, carry):
            m, l, acc = carry
            start = pl.multiple_of(j * tk, tk)
            kj = k_ref[0, pl.ds(start, tk), cols]
            vj = v_ref[0, pl.ds(start, tk), cols]
            s = lax.dot_general(qs, kj, NT_DIMS, preferred_element_type=F32) * scale
            s = jnp.where(k_chunk0 + j * (tk // CHUNK) <= q_chunk, s, MASKED)
            m_new = jnp.maximum(m, jnp.max(s, axis=-1, keepdims=True))
            alpha = jnp.exp(m - m_new)
            p = jnp.exp(s - m_new)
            l = alpha * l + jnp.sum(p, axis=-1, keepdims=True)
            pv = jnp.concatenate(
                [jnp.dot(p[:tq].astype(BF16), vj, preferred_element_type=F32),
                 jnp.dot(p[tq:].astype(BF16), vj, preferred_element_type=F32)], axis=0)
            return m_new, l, alpha * acc + pv

        init = (jnp.full((2 * tq, 1), MASKED, F32), jnp.zeros((2 * tq, 1), F32),
                jnp.zeros((2 * tq, LANES), F32))
        _, l, acc = lax.fori_loop(0, nkb, body, init)
        o = acc / l
        o = o[:tq] - lam * o[tq:]
        ms = jnp.mean(o * o, axis=-1, keepdims=True)
        o = o * lax.rsqrt(ms + NORM_EPS) * g_ref[...] * (1.0 - lam_init)
        o_ref[0, :, cols] = o.astype(o_ref.dtype)


def _diff_attn(bq, bk, bv, lam, g_subln, lam_init):
    b, s, w = bq.shape
    tq = min(DIFF_Q, s)
    kv = pl.BlockSpec((1, s, w), lambda bi, i: (bi, 0, 0))
    qo = pl.BlockSpec((1, tq, w), lambda bi, i: (bi, i, 0))
    return pl.pallas_call(
        functools.partial(_diff_attn_kernel, lam_init=lam_init),
        out_shape=jax.ShapeDtypeStruct((b, s, w), BF16),
        grid=(b, s // tq),
        in_specs=[pl.BlockSpec(memory_space=pltpu.SMEM), qo, kv, kv,
                  pl.BlockSpec((1, B_V_DIM), lambda bi, i: (0, 0))],
        out_specs=qo,
        compiler_params=_params("arbitrary", "arbitrary"),
        name="diff_attn",
    )(lam, bq, bk, bv, g_subln)


def _stack_heads(src_ref, dst_ref, scale):
    tq = src_ref.shape[1]
    lane = lax.broadcasted_iota(jnp.int32, (1, LANES), 1)
    lo = lane < HEAD_DIM
    for hd in range(A_HEADS):
        blk = src_ref[0, :, (hd // 2) * LANES:(hd // 2 + 1) * LANES]
        if scale is not None:
            blk = blk * jnp.asarray(scale, blk.dtype)
        keep = lo if hd % 2 == 0 else jnp.logical_not(lo)
        dst_ref[hd * tq:(hd + 1) * tq, :] = jnp.where(keep, blk, jnp.zeros_like(blk))


def _dsa_attn_kernel(aq_ref, iq_ref, kka_ref, kki_ref, vt_ref, iw_ref, o_ref,
                     qa_ref, qi_ref, key_ref, hi_ref, lo_ref, *, k_sel):
    i = pl.program_id(1)
    tq = aq_ref.shape[1]
    tk = key_ref.shape[1]
    nsel = (i * tq + tq + tk - 1) // tk
    _stack_heads(aq_ref, qa_ref, HEAD_DIM ** -0.5)
    _stack_heads(iq_ref, qi_ref, None)
    sub = lax.broadcasted_iota(jnp.int32, (tk, tq), 0)
    k_chunk0 = sub // CHUNK
    q_chunk = (i * tq + lax.broadcasted_iota(jnp.int32, (tk, tq), 1)) // CHUNK

    def score_block(j, _):
        start = pl.multiple_of(j * tk, tk)
        kj = kki_ref[0, pl.ds(start, tk), :]
        st = lax.dot_general(kj, qi_ref[...], NT_DIMS, preferred_element_type=F32)
        score = jnp.zeros((tk, tq), F32)
        for hd in range(IDX_HEADS):
            rel = jnp.maximum(st[:, hd * tq:(hd + 1) * tq], 0.0)
            score = score + rel * iw_ref[hd:hd + 1, :]
        bits = pltpu.bitcast(score, jnp.int32)
        key = jnp.where(bits < 0, jnp.int32(INT_MIN) - bits, bits)
        key = jnp.where(k_chunk0 + j * (tk // CHUNK) > q_chunk, jnp.int32(INT_MIN), key)
        key_ref[j] = key
        hi_ref[j] = lax.shift_right_arithmetic(key, 16).astype(jnp.int16)
        return 0

    lax.fori_loop(0, nsel, score_block, 0)

    def count(pred):
        def body(j, acc):
            hit = jnp.where(pred(key_ref[j], j), 1, 0).astype(jnp.int32)
            return acc + jnp.sum(hit.reshape(tk // 8, 8, tq), axis=0)
        acc = lax.fori_loop(0, nsel, body, jnp.zeros((8, tq), jnp.int32))
        return jnp.sum(acc, axis=0, keepdims=True)

    half_min = -(2 ** 15)
    one16 = jnp.ones((), BF16)
    zero16 = jnp.zeros((), BF16)

    def search16(ref, want):
        def count_ge(cand):
            c16 = jnp.broadcast_to(cand, (16, tq)).astype(jnp.int16)

            def body(j, acc):
                hit = jnp.where(ref[j].reshape(tk // 16, 16, tq) >= c16[None], one16, zero16)
                parts = [hit[r] for r in range(tk // 16)]
                while len(parts) > 1:
                    parts = [parts[a] + parts[a + 1] for a in range(0, len(parts), 2)]
                return acc + parts[0].astype(F32)
            acc = lax.fori_loop(0, nsel, body, jnp.zeros((16, tq), F32))
            return jnp.sum(acc, axis=0, keepdims=True)

        def bit(t, v):
            cand = v + lax.shift_left(jnp.int32(1), 15 - t)
            return jnp.where(count_ge(cand) >= want, cand, v)
        return lax.fori_loop(0, 16, bit, jnp.full((1, tq), half_min, jnp.int32))

    thr_hi = search16(hi_ref, float(k_sel))
    above = count(lambda key, j: lax.shift_right_arithmetic(key, 16) > thr_hi)

    def low_block(j, _):
        key = key_ref[j]
        low = jnp.bitwise_and(key, 0xFFFF) + half_min
        low = jnp.where(lax.shift_right_arithmetic(key, 16) == thr_hi, low, half_min)
        lo_ref[j] = low.astype(jnp.int16)
        return 0

    lax.fori_loop(0, nsel, low_block, 0)
    thr_lo = search16(lo_ref, (k_sel - above).astype(F32))
    thr = lax.shift_left(thr_hi, 16) + (thr_lo - half_min)
    n_gt = count(lambda key, j: key > thr)
    n_eq = count(lambda key, j: key == thr)
    need = k_sel - n_gt
    tie_break = jnp.logical_and(thr > INT_MIN, n_eq > need)

    def tie_search(_):
        def bit(t, cut):
            cand = cut + lax.shift_left(jnp.int32(1), 15 - t)
            below = count(lambda key, j: jnp.logical_and(key == thr, sub + j * tk < cand))
            return jnp.where(below < need, cand, cut)
        return lax.fori_loop(0, 16, bit, jnp.zeros((1, tq), jnp.int32))

    any_tie = jnp.max(jnp.where(tie_break, 1, 0)) > 0
    cut = lax.cond(any_tie, tie_search, lambda _: jnp.full((1, tq), 2 ** 30, jnp.int32), 0)
    cut = jnp.where(tie_break, cut, 2 ** 30)

    def attn_block(j, carry):
        m, l, acc = carry
        start = pl.multiple_of(j * tk, tk)
        kj = kka_ref[0, pl.ds(start, tk), :]
        st = lax.dot_general(kj, qa_ref[...], NT_DIMS, preferred_element_type=F32)
        key = key_ref[j]
        sel = jnp.logical_or(key > thr, jnp.logical_and(key == thr, sub + j * tk <= cut))
        sel = jnp.logical_and(sel, key > INT_MIN)
        s = jnp.concatenate([jnp.where(sel, st[:, hd * tq:(hd + 1) * tq], MASKED)
                             for hd in range(A_HEADS)], axis=1)
        m_new = jnp.maximum(m, jnp.max(s, axis=0, keepdims=True))
        alpha = jnp.exp(m - m_new)
        p = jnp.exp(s - m_new)
        l = alpha * l + jnp.sum(p, axis=0, keepdims=True)
        acc = alpha * acc + jnp.dot(vt_ref[j], p.astype(BF16), preferred_element_type=F32)
        return m_new, l, acc

    rows = A_HEADS * tq
    init = (jnp.full((1, rows), MASKED, F32), jnp.zeros((1, rows), F32), jnp.zeros((HEAD_DIM, rows), F32))
    _, l, acc = lax.fori_loop(0, nsel, attn_block, init)
    out_t = acc / l
    for g in range(A_HEADS // 2):
        pair = out_t[:, 2 * g * tq:(2 * g + 2) * tq]
        pair = jnp.concatenate([pair[:, :tq], pair[:, tq:]], axis=0)
        o_ref[0, :, g * LANES:(g + 1) * LANES] = pair.T.astype(o_ref.dtype)


def _dsa_attn(aq, iq, kka, kki, v_t, iw_t, k_sel):
    b, s, w = aq.shape
    tq = min(DSA_Q, s)
    tk = v_t.shape[2]
    nq = s // tq
    qo = pl.BlockSpec((1, tq, w), lambda bi, i: (bi, i, 0))
    kv = pl.BlockSpec((1, s, LANES), lambda bi, i: (bi, 0, 0))
    return pl.pallas_call(
        functools.partial(_dsa_attn_kernel, k_sel=k_sel),
        out_shape=jax.ShapeDtypeStruct((b, s, w), BF16),
        grid=(b, nq),
        in_specs=[qo, qo, kv, kv,
                  pl.BlockSpec((s // tk, HEAD_DIM, tk), lambda bi, i: (bi, 0, 0)),
                  pl.BlockSpec((IDX_HEADS, tq), lambda bi, i: (0, bi * nq + i))],
        out_specs=qo,
        scratch_shapes=[pltpu.VMEM((A_HEADS * tq, LANES), BF16),
                        pltpu.VMEM((IDX_HEADS * tq, LANES), BF16),
                        pltpu.VMEM((s // tk, tk, tq), jnp.int32),
                        pltpu.VMEM((s // tk, tk, tq), jnp.int16),
                        pltpu.VMEM((s // tk, tk, tq), jnp.int16)],
        compiler_params=_params("arbitrary", "arbitrary"),
        name="dsa_attn",
    )(aq, iq, kka, kki, v_t, iw_t)


def _pack_halves(y):
    w = y.shape[1] // 2
    r = y.astype(BF16).astype(F32)
    lo = pltpu.bitcast(r[:, :w], jnp.uint32)
    hi = pltpu.bitcast(r[:, w:], jnp.uint32)
    return jnp.bitwise_or(jnp.bitwise_and(hi, jnp.uint32(0xFFFF0000)), lax.shift_right_logical(lo, jnp.uint32(16)))


def _unpack_halves(u):
    lo = pltpu.bitcast(lax.shift_left(u, jnp.uint32(16)), F32)
    hi = pltpu.bitcast(jnp.bitwise_and(u, jnp.uint32(0xFFFF0000)), F32)
    return lo, hi


SLAB = 4


def _store_slab(ref, first, packed):
    rows = packed.shape[0]
    for c in range(SLAB):
        ref[pl.ds(SLAB * first + c, rows, stride=SLAB), :] = packed[:, c * LANES:(c + 1) * LANES]


def _load_slab(ref, first, rows):
    return jnp.concatenate([ref[pl.ds(SLAB * first + c, rows, stride=SLAB), :] for c in range(SLAB)], axis=1)


def _out_proj_kernel(x_ref, ya_ref, yb_ref, wa_ref, wb_ref, g_ref, wr_ref, br_ref,
                     x1_ref, hp_ref, idx_ref, gate_ref, cnt_ref):
    x1 = (x_ref[...] + jnp.dot(ya_ref[...], wa_ref[...], preferred_element_type=F32)
          + jnp.dot(yb_ref[...], wb_ref[...], preferred_element_type=F32))
    x1_ref[...] = x1
    ms = jnp.mean(x1 * x1, axis=-1, keepdims=True)
    h = x1 * lax.rsqrt(ms + NORM_EPS) * g_ref[...]
    _store_slab(hp_ref, 0, _pack_halves(h))
    logits = jnp.dot(h.astype(BF16), wr_ref[...], preferred_element_type=F32) + br_ref[...]
    tm = logits.shape[0]
    lane = lax.broadcasted_iota(jnp.int32, logits.shape, 1)
    work = jnp.where(lane < N_EXPERTS, logits, -jnp.inf)
    val_out = jnp.full(logits.shape, -jnp.inf, F32)
    member = jnp.zeros(logits.shape, F32)
    args = []
    for r in range(TOP_EXPERTS):
        top = jnp.max(work, axis=-1, keepdims=True)
        arg = jnp.min(jnp.where(work == top, lane, LANES), axis=-1, keepdims=True)
        args.append(arg)
        val_out = jnp.where(lane == r, top, val_out)
        member = jnp.where(lane == arg, 1.0, member)
        work = jnp.where(lane == arg, -jnp.inf, work)
    e = jnp.exp(val_out - jnp.max(val_out, axis=-1, keepdims=True))
    gate_ref[...] = e / jnp.sum(e, axis=-1, keepdims=True)
    earlier = jnp.where(lax.broadcasted_iota(jnp.int32, (tm, tm), 1)
                        < lax.broadcasted_iota(jnp.int32, (tm, tm), 0), 1.0, 0.0).astype(BF16)
    before = jnp.dot(earlier, member.astype(BF16), preferred_element_type=F32)
    idx_out = jnp.zeros(logits.shape, jnp.int32)
    for r in range(TOP_EXPERTS):
        rank = jnp.sum(jnp.where(lane == args[r], before, 0.0), axis=-1, keepdims=True)
        idx_out = jnp.where(lane == r, args[r], idx_out)
        idx_out = jnp.where(lane == TOP_EXPERTS + r, rank.astype(jnp.int32), idx_out)
    idx_ref[...] = idx_out
    counts = jnp.sum(member, axis=0, keepdims=True).astype(jnp.int32)
    cnt_ref[0] = jnp.broadcast_to(counts, (8, LANES))


def _out_proj(x2, y_a, y_b, w_o, g_ffn, w_router, b_router):
    n, d = x2.shape
    da = y_a.shape[1]
    wa = w_o[:da].astype(BF16)
    wb = w_o[da:].astype(BF16)
    wr = jnp.concatenate([w_router, jnp.zeros((d, LANES - N_EXPERTS), w_router.dtype)], 1).astype(BF16)
    br = jnp.concatenate([b_router, jnp.zeros((LANES - N_EXPERTS,), F32)]).reshape(1, LANES)
    tm = min(PROJ_ROWS, n)
    row = lambda width: pl.BlockSpec((tm, width), lambda i: (i, 0))
    full = lambda a: pl.BlockSpec(a.shape, lambda i: (0, 0))
    return pl.pallas_call(
        _out_proj_kernel,
        out_shape=(jax.ShapeDtypeStruct((n, d), F32), jax.ShapeDtypeStruct((SLAB * n, LANES), jnp.uint32),
                   jax.ShapeDtypeStruct((n, LANES), jnp.int32), jax.ShapeDtypeStruct((n, LANES), F32),
                   jax.ShapeDtypeStruct((n // tm, 8, LANES), jnp.int32)),
        grid=(n // tm,),
        in_specs=[row(d), row(da), row(y_b.shape[1]), full(wa), full(wb), full(g_ffn), full(wr), full(br)],
        out_specs=(row(d), pl.BlockSpec((SLAB * tm, LANES), lambda i: (i, 0)), row(LANES), row(LANES),
                   pl.BlockSpec((1, 8, LANES), lambda i: (i, 0, 0))),
        compiler_params=_params("arbitrary"),
        name="out_proj",
    )(x2, y_a, y_b, wa, wb, g_ffn, wr, br)


def _moe_plan(top_idx, counts, tile_rows):
    n = top_idx.shape[0]
    rows = MOE_ROWS
    n_blocks = -(-(n * TOP_EXPERTS + N_EXPERTS * (rows - 1)) // rows)
    total = jnp.sum(counts, axis=0)
    padded = ((total + rows - 1) // rows) * rows
    pend = jnp.cumsum(padded)
    base = (pend - padded)[None, :] + jnp.cumsum(counts, axis=0) - counts
    expert = top_idx[:, :TOP_EXPERTS]
    rank = top_idx[:, TOP_EXPERTS:2 * TOP_EXPERTS]
    base_tok = jnp.repeat(base, tile_rows, axis=0)
    hit = expert[:, :, None] == jnp.arange(N_EXPERTS, dtype=jnp.int32)[None, None, :]
    dest = jnp.sum(jnp.where(hit, base_tok[:, None, :], 0), axis=-1) + rank
    block_start = jnp.arange(n_blocks, dtype=jnp.int32) * rows
    block_expert = jnp.minimum(jnp.sum((block_start[:, None] >= pend[None, :]).astype(jnp.int32), axis=1),
                               N_EXPERTS - 1).astype(jnp.int32)
    n_used = (pend[-1] // rows).astype(jnp.int32).reshape(1)
    return dest.astype(jnp.int32).reshape(-1), block_expert, n_used, n_blocks


def _dispatch_kernel(dest_ref, h_ref, xs_in, xs_hbm, sem):
    del xs_in
    per_step = dest_ref.shape[2]
    tm = per_step // TOP_EXPERTS

    def issue(g, _):
        first = g * (ISSUE_GROUP * TOP_EXPERTS)
        slots = [dest_ref[0, 0, first + j] for j in range(ISSUE_GROUP * TOP_EXPERTS)]
        for j, slot in enumerate(slots):
            t = g * ISSUE_GROUP + j // TOP_EXPERTS
            src = h_ref.at[pl.ds(pl.multiple_of(SLAB * t, SLAB), SLAB)]
            dst = xs_hbm.at[pl.ds(pl.multiple_of(SLAB * slot, SLAB), SLAB)]
            pltpu.make_async_copy(src, dst, sem).start(priority=j % 2)
        return 0

    lax.fori_loop(0, tm // ISSUE_GROUP, issue, 0)
    whole = xs_hbm.at[pl.ds(0, SLAB * per_step)]
    pltpu.make_async_copy(whole, whole, sem).wait()


def _moe_dispatch(hp, dest, m):
    n = hp.shape[0] // SLAB
    tm = min(PROJ_ROWS, n)
    per_step = tm * TOP_EXPERTS
    return pl.pallas_call(
        _dispatch_kernel,
        out_shape=jax.ShapeDtypeStruct((SLAB * m, LANES), jnp.uint32),
        grid=(n // tm,),
        in_specs=[pl.BlockSpec((1, 1, per_step), lambda i: (i, 0, 0), memory_space=pltpu.SMEM),
                  pl.BlockSpec((SLAB * tm, LANES), lambda i: (i, 0)), pl.BlockSpec(memory_space=pl.ANY)],
        out_specs=pl.BlockSpec(memory_space=pl.ANY),
        scratch_shapes=[pltpu.SemaphoreType.DMA],
        input_output_aliases={2: 0},
        compiler_params=_params("arbitrary"),
        name="moe_dispatch",
    )(dest.reshape(n // tm, 1, per_step), hp, jnp.zeros((SLAB * m, LANES), jnp.uint32))


def _moe_kernel(bexp_ref, nused_ref, x_ref, wgu_ref, bgu_ref, wd_ref, bd_ref, y_ref):
    del bexp_ref
    rows = x_ref.shape[0] // SLAB
    half = SLAB * LANES
    f = wd_ref.shape[1]

    @pl.when(pl.program_id(0) < nused_ref[0])
    def _():
        lo, hi = _unpack_halves(_load_slab(x_ref, 0, rows))
        gu = (jnp.dot(lo.astype(BF16), wgu_ref[0, :half, :], preferred_element_type=F32)
              + jnp.dot(hi.astype(BF16), wgu_ref[0, half:, :], preferred_element_type=F32)
              + bgu_ref[0])
        glu = jnp.minimum(gu[:, :f], SWIGLU_LIMIT)
        lin = jnp.clip(gu[:, f:], -SWIGLU_LIMIT, SWIGLU_LIMIT)
        act = glu * jax.nn.sigmoid(SWIGLU_ALPHA * glu) * (lin + 1.0)
        y = jnp.dot(act.astype(BF16), wd_ref[0], preferred_element_type=F32) + bd_ref[0]
        _store_slab(y_ref, 0, _pack_halves(y))

    @pl.when(pl.program_id(0) >= nused_ref[0])
    def _():
        y_ref[...] = jnp.zeros(y_ref.shape, y_ref.dtype)


def _moe_experts(xs, block_expert, n_used, w_gu, b_gu, w_down, b_down):
    m = xs.shape[0] // SLAB
    d = 2 * SLAB * LANES
    f = w_down.shape[1]
    rows = MOE_ROWS
    wgu = w_gu.astype(BF16)
    wd = w_down.astype(BF16)
    bgu = b_gu.reshape(N_EXPERTS, 1, 2 * f)
    bd = b_down.reshape(N_EXPERTS, 1, d)
    slots = pl.BlockSpec((SLAB * rows, LANES), lambda i, be, nu: (i, 0))
    by_expert = lambda shape: pl.BlockSpec((1,) + shape, lambda i, be, nu: (be[i], 0, 0))
    return pl.pallas_call(
        _moe_kernel,
        out_shape=jax.ShapeDtypeStruct((SLAB * m, LANES), jnp.uint32),
        grid_spec=pltpu.PrefetchScalarGridSpec(
            num_scalar_prefetch=2,
            grid=(m // rows,),
            in_specs=[slots, by_expert((d, 2 * f)), by_expert((1, 2 * f)),
                      by_expert((f, d)), by_expert((1, d))],
            out_specs=slots),
        compiler_params=_params("arbitrary"),
        name="moe_experts",
    )(block_expert, n_used, xs, wgu, bgu, wd, bd)


def _combine_kernel(dcur_ref, dnext_ref, y_hbm, x_ref, gate_ref, o_ref, ybuf, sems):
    i = pl.program_id(0)
    nsteps = pl.num_programs(0)
    tm = x_ref.shape[0]
    half = SLAB * LANES
    slot = lax.rem(i, 2)

    def issue(dest_ref, buf):
        def body(g, _):
            first = g * (ISSUE_GROUP * TOP_EXPERTS)
            slots = [dest_ref[0, 0, first + j] for j in range(ISSUE_GROUP * TOP_EXPERTS)]
            for j, slot in enumerate(slots):
                t = g * ISSUE_GROUP + j // TOP_EXPERTS
                src = pl.multiple_of(SLAB * slot, SLAB)
                dst = pl.multiple_of(SLAB * ((j % TOP_EXPERTS) * tm + t), SLAB)
                pltpu.make_async_copy(y_hbm.at[pl.ds(src, SLAB)], ybuf.at[buf, pl.ds(dst, SLAB)],
                                      sems.at[buf]).start(priority=j % 2)
            return 0
        lax.fori_loop(0, tm // ISSUE_GROUP, body, 0)

    @pl.when(i == 0)
    def _():
        issue(dcur_ref, 0)

    @pl.when(i + 1 < nsteps)
    def _():
        issue(dnext_ref, 1 - slot)

    pltpu.make_async_copy(y_hbm.at[pl.ds(0, SLAB * TOP_EXPERTS * tm)], ybuf.at[slot], sems.at[slot]).wait()
    lo = x_ref[:, :half]
    hi = x_ref[:, half:]
    for k in range(TOP_EXPERTS):
        a, b = _unpack_halves(_load_slab(ybuf.at[slot], k * tm, tm))
        g = gate_ref[:, k:k + 1]
        lo = lo + a * g
        hi = hi + b * g
    o_ref[:, :half] = lo
    o_ref[:, half:] = hi


def _moe_combine(x1, y_rows, dest, gates):
    n, d = x1.shape
    tm = min(COMBINE_ROWS, n)
    nt = n // tm
    per_step = tm * TOP_EXPERTS
    dest3 = dest.reshape(nt, 1, per_step)
    return pl.pallas_call(
        _combine_kernel,
        out_shape=jax.ShapeDtypeStruct((n, d), F32),
        grid=(nt,),
        in_specs=[pl.BlockSpec((1, 1, per_step), lambda i: (i, 0, 0), memory_space=pltpu.SMEM),
                  pl.BlockSpec((1, 1, per_step), lambda i: (jnp.minimum(i + 1, nt - 1), 0, 0),
                               memory_space=pltpu.SMEM),
                  pl.BlockSpec(memory_space=pl.ANY),
                  pl.BlockSpec((tm, d), lambda i: (i, 0)),
                  pl.BlockSpec((tm, LANES), lambda i: (i, 0))],
        out_specs=pl.BlockSpec((tm, d), lambda i: (i, 0)),
        scratch_shapes=[pltpu.VMEM((2, SLAB * per_step, LANES), jnp.uint32), pltpu.SemaphoreType.DMA((2,))],
        compiler_params=_params("arbitrary"),
        name="moe_combine",
    )(dest3, dest3, y_rows, x1, gates)


def kernel(x, positions, g_attn, w_in, g_qa, g_ka, g_idx_k, lambda_q1, lambda_k1, lambda_q2, lambda_k2, g_qb, g_kb, g_subln, w_o, g_ffn, w_router, b_router, w_gu, b_gu, w_down, b_down):
    b, s, d = x.shape
    n = b * s
    k_sel = min(TOPK_KEYS, s // 4)
    c_tab, s_tab = _rope_tables(positions)
    for layer in range(g_attn.shape[0]):
        lam_init = 0.8 - 0.6 * math.exp(-0.3 * layer)
        x2 = x.reshape(n, d)
        aq, iq, bq, bk, bv, kka, kki, v_t, iw_t = _in_proj(
            x2, g_attn[layer].reshape(1, d), w_in[layer], c_tab, s_tab,
            g_qa[layer], g_ka[layer], g_idx_k[layer], g_qb[layer], g_kb[layer])
        seq = lambda t: t.reshape(b, s, t.shape[-1])
        y_a = _dsa_attn(seq(aq), seq(iq), seq(kka), seq(kki), v_t, iw_t, k_sel)
        lam = (jnp.exp(jnp.sum(lambda_q1[layer].astype(F32) * lambda_k1[layer].astype(F32)))
               - jnp.exp(jnp.sum(lambda_q2[layer].astype(F32) * lambda_k2[layer].astype(F32)))
               + lam_init).reshape(1)
        y_b = _diff_attn(seq(bq), seq(bk), seq(bv), lam, g_subln[layer].reshape(1, B_V_DIM), lam_init)
        x1, hp, top_idx, gates, counts = _out_proj(
            x2, y_a.reshape(n, -1), y_b.reshape(n, -1), w_o[layer], g_ffn[layer].reshape(1, d),
            w_router[layer], b_router[layer])
        dest, block_expert, n_used, n_blocks = _moe_plan(
            top_idx, counts[:, 0, :N_EXPERTS], n // counts.shape[0])
        xs = _moe_dispatch(hp, dest, n_blocks * MOE_ROWS)
        y_rows = _moe_experts(xs, block_expert, n_used, w_gu[layer], b_gu[layer], w_down[layer], b_down[layer])
        x = _moe_combine(x1, y_rows, dest, gates).reshape(b, s, d)
    return x
```

```python
import functools
import math

import jax
import jax.numpy as jnp
import numpy as np
from jax import lax
from jax.experimental import pallas as pl
from jax.experimental.pallas import tpu as pltpu

CHUNK = 64
ROPE_THETA = 500000.0
NORM_EPS = 1e-6
A_HEADS = 8
HEAD_DIM = 64
IDX_HEADS = 8
TOPK_KEYS = 256
B_HEADS = 4
B_V_DIM = 128
ROT_DIM = HEAD_DIM // 4
ROT_HALF = ROT_DIM // 2
N_EXPERTS = 32
TOP_EXPERTS = 4
SWIGLU_LIMIT = 7.0
SWIGLU_ALPHA = 1.702

LANES = 128
VMEM_LIMIT = 48 * 1024 * 1024

PROJ_ROWS = 512
DIFF_Q = 256
DIFF_K = 512
DSA_Q = 128
DSA_K = 256
MOE_ROWS = 256
COMBINE_ROWS = 256
ISSUE_GROUP = 4

MASKED = -1e30
INT_MIN = -(2 ** 31)

BF16 = jnp.bfloat16
F32 = jnp.float32
NT_DIMS = (((1,), (1,)), ((), ()))


def _params(*sem):
    return pltpu.CompilerParams(dimension_semantics=sem, vmem_limit_bytes=VMEM_LIMIT)


def _rope_kernel(invf_ref, pos_ref, cos_ref, sin_ref):
    f = pl.program_id(0)
    ang = pos_ref[...].astype(F32) * invf_ref[f]
    cos_ref[0] = jnp.cos(ang)
    sin_ref[0] = jnp.sin(ang)


def _rope_tables(positions):
    b, s = positions.shape
    inv_freq = ROPE_THETA ** (-jnp.arange(0, ROT_DIM, 2, dtype=F32) / ROT_DIM)
    cos_t, sin_t = pl.pallas_call(
        _rope_kernel,
        out_shape=(jax.ShapeDtypeStruct((ROT_HALF, b, s), F32),) * 2,
        grid=(ROT_HALF,),
        in_specs=[pl.BlockSpec(memory_space=pltpu.SMEM),
                  pl.BlockSpec((b, s), lambda f: (0, 0))],
        out_specs=(pl.BlockSpec((1, b, s), lambda f: (f, 0, 0)),) * 2,
        compiler_params=_params("arbitrary"),
        name="rope_tables",
    )(inv_freq, positions)
    n = b * s
    cos8 = jnp.transpose(cos_t, (1, 2, 0)).reshape(n, ROT_HALF)
    sin8 = jnp.transpose(sin_t, (1, 2, 0)).reshape(n, ROT_HALF)
    ones = jnp.ones((n, HEAD_DIM - ROT_DIM), F32)
    zeros = jnp.zeros((n, HEAD_DIM - ROT_DIM), F32)
    c_tab = jnp.concatenate([cos8, cos8, ones], axis=1)
    s_tab = jnp.concatenate([-sin8, sin8, zeros], axis=1)
    return jnp.tile(c_tab, (1, 2)), jnp.tile(s_tab, (1, 2))


def _segment_rms(y, gain):
    lane = lax.broadcasted_iota(jnp.int32, (1, LANES), 1)
    lo = lane < HEAD_DIM
    y2 = y * y
    s0 = jnp.sum(jnp.where(lo, y2, 0.0), axis=-1, keepdims=True)
    s1 = jnp.sum(jnp.where(lo, 0.0, y2), axis=-1, keepdims=True)
    ms = jnp.where(lo, s0, s1) * (1.0 / HEAD_DIM)
    return y * lax.rsqrt(ms + NORM_EPS) * gain


def _rope128(y, c_tab, s_tab):
    lane = lax.broadcasted_iota(jnp.int32, (1, LANES), 1)
    first = (lane % HEAD_DIM) < ROT_HALF
    up = pltpu.roll(y, LANES - ROT_HALF, 1)
    down = pltpu.roll(y, ROT_HALF, 1)
    return y * c_tab + jnp.where(first, up, down) * s_tab


def _in_proj_kernel(x_ref, g_ref, w_ref, c_ref, s_ref, gains_ref,
                    aq_ref, iq_ref, bq_ref, bk_ref, bv_ref, kka_ref, kki_ref, vt_ref, iw_ref):
    x = x_ref[...]
    ms = jnp.mean(x * x, axis=-1, keepdims=True)
    h = (x * lax.rsqrt(ms + NORM_EPS) * g_ref[...]).astype(BF16)
    c_tab = c_ref[...]
    s_tab = s_ref[...]
    wide = 4 * LANES

    def group(col, out_ref, gain_row, rope):
        acc = jnp.dot(h, w_ref[:, col:col + wide], preferred_element_type=F32)
        for t in range(4):
            y = acc[:, t * LANES:(t + 1) * LANES]
            if gain_row is not None:
                y = _segment_rms(y, gains_ref[gain_row:gain_row + 1, :])
            if rope:
                y = _rope128(y, c_tab, s_tab)
            out_ref[:, t * LANES:(t + 1) * LANES] = y.astype(out_ref.dtype)

    group(0 * wide, aq_ref, 0, True)
    group(1 * wide, iq_ref, None, True)
    group(2 * wide, bq_ref, 1, True)
    group(3 * wide, bk_ref, 2, True)
    group(4 * wide, bv_ref, None, False)
    acc = jnp.dot(h, w_ref[:, 5 * wide:6 * wide], preferred_element_type=F32)
    ka = _rope128(_segment_rms(acc[:, 0:LANES], gains_ref[3:4, :]), c_tab, s_tab)
    ki = _rope128(_segment_rms(acc[:, LANES:2 * LANES], gains_ref[4:5, :]), c_tab, s_tab)
    kka_ref[...] = ka.astype(BF16)
    kki_ref[...] = ki.astype(BF16)
    tk = vt_ref.shape[2]
    for c in range(vt_ref.shape[0]):
        v_t = acc[c * tk:(c + 1) * tk, 2 * LANES:3 * LANES].T
        vt_ref[c] = v_t[:HEAD_DIM].astype(BF16)
    iw_t = (acc[:, 3 * LANES:4 * LANES] * ((IDX_HEADS * HEAD_DIM) ** -0.5)).T
    iw_ref[...] = iw_t[:IDX_HEADS]


def _in_proj(x2, g_attn, w_in, c_tab, s_tab, g_qa, g_ka, g_idx_k, g_qb, g_kb):
    n, d = x2.shape
    ha = A_HEADS * HEAD_DIM
    o = np.cumsum([0, ha, HEAD_DIM, HEAD_DIM, IDX_HEADS * HEAD_DIM, HEAD_DIM, IDX_HEADS,
                   B_HEADS * 2 * HEAD_DIM, B_HEADS * 2 * HEAD_DIM, B_HEADS * B_V_DIM])
    aq, ak, av, iq, ik, iw, bq, bk, bv = [w_in[:, o[i]:o[i + 1]] for i in range(9)]
    pad = jnp.zeros((d, LANES - IDX_HEADS), w_in.dtype)
    w = jnp.concatenate([aq, iq, bq, bk, bv, ak, ak, ik, ik, av, av, iw, pad], axis=1).astype(BF16)
    two = lambda g: jnp.tile(g.reshape(1, HEAD_DIM), (1, 2))
    gains = jnp.concatenate([two(g_qa), two(g_qb), two(g_kb), two(g_ka), two(g_idx_k),
                             jnp.zeros((3, LANES), F32)], axis=0)
    tm = min(PROJ_ROWS, n)
    tk = min(DSA_K, tm)
    wide = 4 * LANES
    row = lambda width: pl.BlockSpec((tm, width), lambda i: (i, 0))
    full = lambda a: pl.BlockSpec(a.shape, lambda i: (0, 0))
    outs = pl.pallas_call(
        _in_proj_kernel,
        out_shape=(jax.ShapeDtypeStruct((n, wide), BF16),) * 5
        + (jax.ShapeDtypeStruct((n, LANES), BF16),) * 2
        + (jax.ShapeDtypeStruct((n // tk, HEAD_DIM, tk), BF16),
           jax.ShapeDtypeStruct((IDX_HEADS, n), F32)),
        grid=(n // tm,),
        in_specs=[row(d), full(g_attn), full(w), row(LANES), row(LANES), full(gains)],
        out_specs=(row(wide),) * 5 + (row(LANES),) * 2
        + (pl.BlockSpec((tm // tk, HEAD_DIM, tk), lambda i: (i, 0, 0)),
           pl.BlockSpec((IDX_HEADS, tm), lambda i: (0, i))),
        compiler_params=_params("arbitrary"),
        name="in_proj",
    )(x2, g_attn, w, c_tab, s_tab, gains)
    return outs


def _diff_attn_kernel(lam_ref, q_ref, k_ref, v_ref, g_ref, o_ref, vt_ref, *, lam_init):
    i = pl.program_id(1)
    tq = q_ref.shape[1]
    tk = vt_ref.shape[3]
    lam = lam_ref[0]
    nkb = (i * tq + tq + tk - 1) // tk
    lane = lax.broadcasted_iota(jnp.int32, (1, LANES), 1)
    lo = lane < HEAD_DIM
    k_chunk0 = lax.broadcasted_iota(jnp.int32, (tk, tq), 0) // CHUNK
    q_chunk = (i * tq + lax.broadcasted_iota(jnp.int32, (tk, tq), 1)) // CHUNK

    @pl.when(i == 0)
    def _():
        for hd in range(B_HEADS):
            for c in range(vt_ref.shape[1]):
                blk = v_ref[0, c * tk:(c + 1) * tk, hd * LANES:(hd + 1) * LANES]
                vt_ref[hd, c] = blk.astype(F32).T.astype(BF16)

    qs = []
    for hd in range(B_HEADS):
        qh = q_ref[0, :, hd * LANES:(hd + 1) * LANES] * jnp.asarray(HEAD_DIM ** -0.5, BF16)
        zero = jnp.zeros_like(qh)
        qs.append(jnp.concatenate([jnp.where(lo, qh, zero), jnp.where(lo, zero, qh)], axis=0))

    def body(j, carry):
        start = pl.multiple_of(j * tk, tk)
        ok = k_chunk0 + j * (tk // CHUNK) <= q_chunk
        out = []
        for hd in range(B_HEADS):
            m, l, acc = carry[hd]
            kj = k_ref[0, pl.ds(start, tk), hd * LANES:(hd + 1) * LANES]
            st = lax.dot_general(kj, qs[hd], NT_DIMS, preferred_element_type=F32)
            s = jnp.concatenate([jnp.where(ok, st[:, :tq], MASKED), jnp.where(ok, st[:, tq:], MASKED)],
                                axis=1)
            m_new = jnp.maximum(m, jnp.max(s, axis=0, keepdims=True))
            alpha = jnp.exp(m - m_new)
            p = jnp.exp(s - m_new)
            l = alpha * l + jnp.sum(p, axis=0, keepdims=True)
            acc = alpha * acc + jnp.dot(vt_ref[hd, j], p.astype(BF16), preferred_element_type=F32)
            out.append((m_new, l, acc))
        return tuple(out)

    init = tuple((jnp.full((1, 2 * tq), MASKED, F32), jnp.zeros((1, 2 * tq), F32),
                  jnp.zeros((B_V_DIM, 2 * tq), F32)) for _ in range(B_HEADS))
    final = lax.fori_loop(0, nkb, body, init)
    for hd in range(B_HEADS):
        cols = slice(hd * LANES, (hd + 1) * LANES)
        _, l, acc = final[hd]
        o = acc / l
        o = o[:, :tq] - lam * o[:, tq:]
        ms = jnp.mean(o * o, axis=0, keepdims=True)
        o = o * lax.rsqrt(ms + NORM_EPS) * g_ref[...] * (1.0 - lam_init)
        o_ref[0, :, cols] = o.T.astype(o_ref.dtype)


def _diff_attn(bq, bk, bv, lam, g_subln, lam_init):
    b, s, w = bq.shape
    tq = min(DIFF_Q, s)
    tk = min(DIFF_K, s)
    kv = pl.BlockSpec((1, s, w), lambda bi, i: (bi, 0, 0))
    qo = pl.BlockSpec((1, tq, w), lambda bi, i: (bi, i, 0))
    return pl.pallas_call(
        functools.partial(_diff_attn_kernel, lam_init=lam_init),
        out_shape=jax.ShapeDtypeStruct((b, s, w), BF16),
        grid=(b, s // tq),
        in_specs=[pl.BlockSpec(memory_space=pltpu.SMEM), qo, kv, kv,
                  pl.BlockSpec((B_V_DIM, 1), lambda bi, i: (0, 0))],
        out_specs=qo,
        scratch_shapes=[pltpu.VMEM((B_HEADS, s // tk, B_V_DIM, tk), BF16)],
        compiler_params=_params("arbitrary", "arbitrary"),
        name="diff_attn",
    )(lam, bq, bk, bv, g_subln.reshape(B_V_DIM, 1))


def _stack_heads(src_ref, dst_ref, scale):
    tq = src_ref.shape[1]
    lane = lax.broadcasted_iota(jnp.int32, (1, LANES), 1)
    lo = lane < HEAD_DIM
    for hd in range(A_HEADS):
        blk = src_ref[0, :, (hd // 2) * LANES:(hd // 2 + 1) * LANES]
        if scale is not None:
            blk = blk * jnp.asarray(scale, blk.dtype)
        keep = lo if hd % 2 == 0 else jnp.logical_not(lo)
        dst_ref[hd * tq:(hd + 1) * tq, :] = jnp.where(keep, blk, jnp.zeros_like(blk))


def _dsa_attn_kernel(aq_ref, iq_ref, kka_ref, kki_ref, vt_ref, iw_ref, o_ref,
                     qa_ref, qi_ref, key_ref, *, k_sel):
    i = pl.program_id(1)
    tq = aq_ref.shape[1]
    tk = key_ref.shape[1]
    nsel = (i * tq + tq + tk - 1) // tk
    _stack_heads(aq_ref, qa_ref, HEAD_DIM ** -0.5)
    _stack_heads(iq_ref, qi_ref, None)
    sub = lax.broadcasted_iota(jnp.int32, (tk, tq), 0)
    k_chunk0 = sub // CHUNK
    q_chunk = (i * tq + lax.broadcasted_iota(jnp.int32, (tk, tq), 1)) // CHUNK

    def score_block(j, _):
        start = pl.multiple_of(j * tk, tk)
        kj = kki_ref[0, pl.ds(start, tk), :]
        st = lax.dot_general(kj, qi_ref[...], NT_DIMS, preferred_element_type=F32)
        score = jnp.zeros((tk, tq), F32)
        for hd in range(IDX_HEADS):
            rel = jnp.maximum(st[:, hd * tq:(hd + 1) * tq], 0.0)
            score = score + rel * iw_ref[hd:hd + 1, :]
        bits = pltpu.bitcast(score, jnp.int32)
        key = jnp.where(bits < 0, jnp.int32(INT_MIN) - bits, bits)
        key_ref[j] = jnp.where(k_chunk0 + j * (tk // CHUNK) > q_chunk, jnp.int32(INT_MIN), key)
        return 0

    lax.fori_loop(0, nsel, score_block, 0)

    def count(pred):
        def body(j, acc):
            hit = jnp.where(pred(key_ref[j], j), 1, 0).astype(jnp.int32)
            return acc + jnp.sum(hit.reshape(tk // 8, 8, tq), axis=0)
        acc = lax.fori_loop(0, nsel, body, jnp.zeros((8, tq), jnp.int32))
        return jnp.sum(acc, axis=0, keepdims=True)

    def search_bit(t, thr):
        cand = thr + lax.shift_left(jnp.int32(1), 31 - t)
        cnt = count(lambda key, j: key >= cand)
        return jnp.where(cnt >= k_sel, cand, thr)

    thr = lax.fori_loop(0, 32, search_bit, jnp.full((1, tq), INT_MIN, jnp.int32))
    n_gt = count(lambda key, j: key > thr)
    n_eq = count(lambda key, j: key == thr)
    need = k_sel - n_gt
    tie_break = jnp.logical_and(thr > INT_MIN, n_eq > need)

    def tie_search(_):
        def bit(t, cut):
            cand = cut + lax.shift_left(jnp.int32(1), 15 - t)
            below = count(lambda key, j: jnp.logical_and(key == thr, sub + j * tk < cand))
            return jnp.where(below < need, cand, cut)
        return lax.fori_loop(0, 16, bit, jnp.zeros((1, tq), jnp.int32))

    any_tie = jnp.max(jnp.where(tie_break, 1, 0)) > 0
    cut = lax.cond(any_tie, tie_search, lambda _: jnp.full((1, tq), 2 ** 30, jnp.int32), 0)
    cut = jnp.where(tie_break, cut, 2 ** 30)

    parts = 1
    part_heads = A_HEADS // parts
    part_rows = part_heads * tq

    def attn_block(j, carry):
        start = pl.multiple_of(j * tk, tk)
        kj = kka_ref[0, pl.ds(start, tk), :]
        key = key_ref[j]
        sel = jnp.logical_or(key > thr, jnp.logical_and(key == thr, sub + j * tk <= cut))
        sel = jnp.logical_and(sel, key > INT_MIN)
        out = []
        for part in range(parts):
            m, l, acc = carry[part]
            q_part = qa_ref[part * part_rows:(part + 1) * part_rows, :]
            st = lax.dot_general(kj, q_part, NT_DIMS, preferred_element_type=F32)
            s = jnp.concatenate([jnp.where(sel, st[:, hd * tq:(hd + 1) * tq], MASKED)
                                 for hd in range(part_heads)], axis=1)
            m_new = jnp.maximum(m, jnp.max(s, axis=0, keepdims=True))
            alpha = jnp.exp(m - m_new)
            p = jnp.exp(s - m_new)
            l = alpha * l + jnp.sum(p, axis=0, keepdims=True)
            acc = alpha * acc + jnp.dot(vt_ref[j], p.astype(BF16), preferred_element_type=F32)
            out.append((m_new, l, acc))
        return tuple(out)

    init = tuple((jnp.full((1, part_rows), MASKED, F32), jnp.zeros((1, part_rows), F32),
                  jnp.zeros((HEAD_DIM, part_rows), F32)) for _ in range(parts))
    final = lax.fori_loop(0, nsel, attn_block, init)
    out_t = jnp.concatenate([acc / l for _, l, acc in final], axis=1)
    for g in range(A_HEADS // 2):
        pair = out_t[:, 2 * g * tq:(2 * g + 2) * tq]
        pair = jnp.concatenate([pair[:, :tq], pair[:, tq:]], axis=0)
        o_ref[0, :, g * LANES:(g + 1) * LANES] = pair.T.astype(o_ref.dtype)


def _dsa_attn(aq, iq, kka, kki, v_t, iw_t, k_sel):
    b, s, w = aq.shape
    tq = min(DSA_Q, s)
    tk = v_t.shape[2]
    nq = s // tq
    qo = pl.BlockSpec((1, tq, w), lambda bi, i: (bi, i, 0))
    kv = pl.BlockSpec((1, s, LANES), lambda bi, i: (bi, 0, 0))
    return pl.pallas_call(
        functools.partial(_dsa_attn_kernel, k_sel=k_sel),
        out_shape=jax.ShapeDtypeStruct((b, s, w), BF16),
        grid=(b, nq),
        in_specs=[qo, qo, kv, kv,
                  pl.BlockSpec((s // tk, HEAD_DIM, tk), lambda bi, i: (bi, 0, 0)),
                  pl.BlockSpec((IDX_HEADS, tq), lambda bi, i: (0, bi * nq + i))],
        out_specs=qo,
        scratch_shapes=[pltpu.VMEM((A_HEADS * tq, LANES), BF16),
                        pltpu.VMEM((IDX_HEADS * tq, LANES), BF16),
                        pltpu.VMEM((s // tk, tk, tq), jnp.int32)],
        compiler_params=_params("arbitrary", "arbitrary"),
        name="dsa_attn",
    )(aq, iq, kka, kki, v_t, iw_t)


def _pack_halves(y):
    w = y.shape[1] // 2
    r = y.astype(BF16).astype(F32)
    lo = pltpu.bitcast(r[:, :w], jnp.uint32)
    hi = pltpu.bitcast(r[:, w:], jnp.uint32)
    return jnp.bitwise_or(jnp.bitwise_and(hi, jnp.uint32(0xFFFF0000)), lax.shift_right_logical(lo, jnp.uint32(16)))


def _unpack_halves(u):
    lo = pltpu.bitcast(lax.shift_left(u, jnp.uint32(16)), F32)
    hi = pltpu.bitcast(jnp.bitwise_and(u, jnp.uint32(0xFFFF0000)), F32)
    return lo, hi


SLAB = 4


def _store_slab(ref, first, packed):
    rows = packed.shape[0]
    for c in range(SLAB):
        ref[pl.ds(SLAB * first + c, rows, stride=SLAB), :] = packed[:, c * LANES:(c + 1) * LANES]


def _load_slab(ref, first, rows):
    return jnp.concatenate([ref[pl.ds(SLAB * first + c, rows, stride=SLAB), :] for c in range(SLAB)], axis=1)


def _out_proj_kernel(x_ref, ya_ref, yb_ref, wa_ref, wb_ref, g_ref, wr_ref, br_ref,
                     x1_ref, hp_ref, idx_ref, gate_ref, cnt_ref):
    x1 = (x_ref[...] + jnp.dot(ya_ref[...], wa_ref[...], preferred_element_type=F32)
          + jnp.dot(yb_ref[...], wb_ref[...], preferred_element_type=F32))
    x1_ref[...] = x1
    ms = jnp.mean(x1 * x1, axis=-1, keepdims=True)
    h = x1 * lax.rsqrt(ms + NORM_EPS) * g_ref[...]
    _store_slab(hp_ref, 0, _pack_halves(h))
    logits = jnp.dot(h.astype(BF16), wr_ref[...], preferred_element_type=F32) + br_ref[...]
    tm = logits.shape[0]
    lane = lax.broadcasted_iota(jnp.int32, logits.shape, 1)
    work = jnp.where(lane < N_EXPERTS, logits, -jnp.inf)
    val_out = jnp.full(logits.shape, -jnp.inf, F32)
    member = jnp.zeros(logits.shape, F32)
    args = []
    for r in range(TOP_EXPERTS):
        top = jnp.max(work, axis=-1, keepdims=True)
        arg = jnp.min(jnp.where(work == top, lane, LANES), axis=-1, keepdims=True)
        args.append(arg)
        val_out = jnp.where(lane == r, top, val_out)
        member = jnp.where(lane == arg, 1.0, member)
        work = jnp.where(lane == arg, -jnp.inf, work)
    e = jnp.exp(val_out - jnp.max(val_out, axis=-1, keepdims=True))
    gate_ref[...] = e / jnp.sum(e, axis=-1, keepdims=True)
    earlier = jnp.where(lax.broadcasted_iota(jnp.int32, (tm, tm), 1)
                        < lax.broadcasted_iota(jnp.int32, (tm, tm), 0), 1.0, 0.0).astype(BF16)
    before = jnp.dot(earlier, member.astype(BF16), preferred_element_type=F32)
    idx_out = jnp.zeros(logits.shape, jnp.int32)
    for r in range(TOP_EXPERTS):
        rank = jnp.sum(jnp.where(lane == args[r], before, 0.0), axis=-1, keepdims=True)
        idx_out = jnp.where(lane == r, args[r], idx_out)
        idx_out = jnp.where(lane == TOP_EXPERTS + r, rank.astype(jnp.int32), idx_out)
    idx_ref[...] = idx_out
    counts = jnp.sum(member, axis=0, keepdims=True).astype(jnp.int32)
    cnt_ref[0] = jnp.broadcast_to(counts, (8, LANES))


def _out_proj(x2, y_a, y_b, w_o, g_ffn, w_router, b_router):
    n, d = x2.shape
    da = y_a.shape[1]
    wa = w_o[:da].astype(BF16)
    wb = w_o[da:].astype(BF16)
    wr = jnp.concatenate([w_router, jnp.zeros((d, LANES - N_EXPERTS), w_router.dtype)], 1).astype(BF16)
    br = jnp.concatenate([b_router, jnp.zeros((LANES - N_EXPERTS,), F32)]).reshape(1, LANES)
    tm = min(PROJ_ROWS, n)
    row = lambda width: pl.BlockSpec((tm, width), lambda i: (i, 0))
    full = lambda a: pl.BlockSpec(a.shape, lambda i: (0, 0))
    return pl.pallas_call(
        _out_proj_kernel,
        out_shape=(jax.ShapeDtypeStruct((n, d), F32), jax.ShapeDtypeStruct((SLAB * n, LANES), jnp.uint32),
                   jax.ShapeDtypeStruct((n, LANES), jnp.int32), jax.ShapeDtypeStruct((n, LANES), F32),
                   jax.ShapeDtypeStruct((n // tm, 8, LANES), jnp.int32)),
        grid=(n // tm,),
        in_specs=[row(d), row(da), row(y_b.shape[1]), full(wa), full(wb), full(g_ffn), full(wr), full(br)],
        out_specs=(row(d), pl.BlockSpec((SLAB * tm, LANES), lambda i: (i, 0)), row(LANES), row(LANES),
                   pl.BlockSpec((1, 8, LANES), lambda i: (i, 0, 0))),
        compiler_params=_params("arbitrary"),
        name="out_proj",
    )(x2, y_a, y_b, wa, wb, g_ffn, wr, br)


def _moe_plan(top_idx, counts, tile_rows):
    n = top_idx.shape[0]
    rows = MOE_ROWS
    n_blocks = -(-(n * TOP_EXPERTS + N_EXPERTS * (rows - 1)) // rows)
    total = jnp.sum(counts, axis=0)
    padded = ((total + rows - 1) // rows) * rows
    pend = jnp.cumsum(padded)
    base = (pend - padded)[None, :] + jnp.cumsum(counts, axis=0) - counts
    expert = top_idx[:, :TOP_EXPERTS]
    rank = top_idx[:, TOP_EXPERTS:2 * TOP_EXPERTS]
    base_tok = jnp.repeat(base, tile_rows, axis=0)
    hit = expert[:, :, None] == jnp.arange(N_EXPERTS, dtype=jnp.int32)[None, None, :]
    dest = jnp.sum(jnp.where(hit, base_tok[:, None, :], 0), axis=-1) + rank
    block_start = jnp.arange(n_blocks, dtype=jnp.int32) * rows
    block_expert = jnp.minimum(jnp.sum((block_start[:, None] >= pend[None, :]).astype(jnp.int32), axis=1),
                               N_EXPERTS - 1).astype(jnp.int32)
    n_used = (pend[-1] // rows).astype(jnp.int32).reshape(1)
    return dest.astype(jnp.int32).reshape(-1), block_expert, n_used, n_blocks


def _dispatch_kernel(dest_ref, h_ref, xs_in, xs_hbm, sem):
    del xs_in
    per_step = dest_ref.shape[2]
    tm = per_step // TOP_EXPERTS

    def issue(g, _):
        first = g * (ISSUE_GROUP * TOP_EXPERTS)
        slots = [dest_ref[0, 0, first + j] for j in range(ISSUE_GROUP * TOP_EXPERTS)]
        for j, slot in enumerate(slots):
            t = g * ISSUE_GROUP + j // TOP_EXPERTS
            src = h_ref.at[pl.ds(pl.multiple_of(SLAB * t, SLAB), SLAB)]
            dst = xs_hbm.at[pl.ds(pl.multiple_of(SLAB * slot, SLAB), SLAB)]
            pltpu.make_async_copy(src, dst, sem).start(priority=j % 2)
        return 0

    lax.fori_loop(0, tm // ISSUE_GROUP, issue, 0)
    whole = xs_hbm.at[pl.ds(0, SLAB * per_step)]
    pltpu.make_async_copy(whole, whole, sem).wait()


def _moe_dispatch(hp, dest, m):
    n = hp.shape[0] // SLAB
    tm = min(PROJ_ROWS, n)
    per_step = tm * TOP_EXPERTS
    return pl.pallas_call(
        _dispatch_kernel,
        out_shape=jax.ShapeDtypeStruct((SLAB * m, LANES), jnp.uint32),
        grid=(n // tm,),
        in_specs=[pl.BlockSpec((1, 1, per_step), lambda i: (i, 0, 0), memory_space=pltpu.SMEM),
                  pl.BlockSpec((SLAB * tm, LANES), lambda i: (i, 0)), pl.BlockSpec(memory_space=pl.ANY)],
        out_specs=pl.BlockSpec(memory_space=pl.ANY),
        scratch_shapes=[pltpu.SemaphoreType.DMA],
        input_output_aliases={2: 0},
        compiler_params=_params("arbitrary"),
        name="moe_dispatch",
    )(dest.reshape(n // tm, 1, per_step), hp, jnp.zeros((SLAB * m, LANES), jnp.uint32))


def _moe_kernel(bexp_ref, nused_ref, x_ref, wgu_ref, bgu_ref, wd_ref, bd_ref, y_ref):
    del bexp_ref
    rows = x_ref.shape[0] // SLAB
    half = SLAB * LANES
    f = wd_ref.shape[1]

    @pl.when(pl.program_id(0) < nused_ref[0])
    def _():
        lo, hi = _unpack_halves(_load_slab(x_ref, 0, rows))
        gu = (jnp.dot(lo.astype(BF16), wgu_ref[0, :half, :], preferred_element_type=F32)
              + jnp.dot(hi.astype(BF16), wgu_ref[0, half:, :], preferred_element_type=F32)
              + bgu_ref[0])
        glu = jnp.minimum(gu[:, :f], SWIGLU_LIMIT)
        lin = jnp.clip(gu[:, f:], -SWIGLU_LIMIT, SWIGLU_LIMIT)
        act = glu * jax.nn.sigmoid(SWIGLU_ALPHA * glu) * (lin + 1.0)
        y = jnp.dot(act.astype(BF16), wd_ref[0], preferred_element_type=F32) + bd_ref[0]
        _store_slab(y_ref, 0, _pack_halves(y))

    @pl.when(pl.program_id(0) >= nused_ref[0])
    def _():
        y_ref[...] = jnp.zeros(y_ref.shape, y_ref.dtype)


def _moe_experts(xs, block_expert, n_used, w_gu, b_gu, w_down, b_down):
    m = xs.shape[0] // SLAB
    d = 2 * SLAB * LANES
    f = w_down.shape[1]
    rows = MOE_ROWS
    wgu = w_gu.astype(BF16)
    wd = w_down.astype(BF16)
    bgu = b_gu.reshape(N_EXPERTS, 1, 2 * f)
    bd = b_down.reshape(N_EXPERTS, 1, d)
    slots = pl.BlockSpec((SLAB * rows, LANES), lambda i, be, nu: (i, 0))
    by_expert = lambda shape: pl.BlockSpec((1,) + shape, lambda i, be, nu: (be[i], 0, 0))
    return pl.pallas_call(
        _moe_kernel,
        out_shape=jax.ShapeDtypeStruct((SLAB * m, LANES), jnp.uint32),
        grid_spec=pltpu.PrefetchScalarGridSpec(
            num_scalar_prefetch=2,
            grid=(m // rows,),
            in_specs=[slots, by_expert((d, 2 * f)), by_expert((1, 2 * f)),
                      by_expert((f, d)), by_expert((1, d))],
            out_specs=slots),
        compiler_params=_params("arbitrary"),
        name="moe_experts",
    )(block_expert, n_used, xs, wgu, bgu, wd, bd)


def _combine_kernel(dcur_ref, dnext_ref, y_hbm, x_ref, gate_ref, o_ref, ybuf, sems):
    i = pl.program_id(0)
    nsteps = pl.num_programs(0)
    tm = x_ref.shape[0]
    half = SLAB * LANES
    slot = lax.rem(i, 2)

    def issue(dest_ref, buf):
        def body(g, _):
            first = g * (ISSUE_GROUP * TOP_EXPERTS)
            slots = [dest_ref[0, 0, first + j] for j in range(ISSUE_GROUP * TOP_EXPERTS)]
            for j, slot in enumerate(slots):
                t = g * ISSUE_GROUP + j // TOP_EXPERTS
                src = pl.multiple_of(SLAB * slot, SLAB)
                dst = pl.multiple_of(SLAB * ((j % TOP_EXPERTS) * tm + t), SLAB)
                pltpu.make_async_copy(y_hbm.at[pl.ds(src, SLAB)], ybuf.at[buf, pl.ds(dst, SLAB)],
                                      sems.at[buf]).start(priority=j % 2)
            return 0
        lax.fori_loop(0, tm // ISSUE_GROUP, body, 0)

    @pl.when(i == 0)
    def _():
        issue(dcur_ref, 0)

    @pl.when(i + 1 < nsteps)
    def _():
        issue(dnext_ref, 1 - slot)

    pltpu.make_async_copy(y_hbm.at[pl.ds(0, SLAB * TOP_EXPERTS * tm)], ybuf.at[slot], sems.at[slot]).wait()
    lo = x_ref[:, :half]
    hi = x_ref[:, half:]
    for k in range(TOP_EXPERTS):
        a, b = _unpack_halves(_load_slab(ybuf.at[slot], k * tm, tm))
        g = gate_ref[:, k:k + 1]
        lo = lo + a * g
        hi = hi + b * g
    o_ref[:, :half] = lo
    o_ref[:, half:] = hi


def _moe_combine(x1, y_rows, dest, gates):
    n, d = x1.shape
    tm = min(COMBINE_ROWS, n)
    nt = n // tm
    per_step = tm * TOP_EXPERTS
    dest3 = dest.reshape(nt, 1, per_step)
    return pl.pallas_call(
        _combine_kernel,
        out_shape=jax.ShapeDtypeStruct((n, d), F32),
        grid=(nt,),
        in_specs=[pl.BlockSpec((1, 1, per_step), lambda i: (i, 0, 0), memory_space=pltpu.SMEM),
                  pl.BlockSpec((1, 1, per_step), lambda i: (jnp.minimum(i + 1, nt - 1), 0, 0),
                               memory_space=pltpu.SMEM),
                  pl.BlockSpec(memory_space=pl.ANY),
                  pl.BlockSpec((tm, d), lambda i: (i, 0)),
                  pl.BlockSpec((tm, LANES), lambda i: (i, 0))],
        out_specs=pl.BlockSpec((tm, d), lambda i: (i, 0)),
        scratch_shapes=[pltpu.VMEM((2, SLAB * per_step, LANES), jnp.uint32), pltpu.SemaphoreType.DMA((2,))],
        compiler_params=_params("arbitrary"),
        name="moe_combine",
    )(dest3, dest3, y_rows, x1, gates)


def kernel(x, positions, g_attn, w_in, g_qa, g_ka, g_idx_k, lambda_q1, lambda_k1, lambda_q2, lambda_k2, g_qb, g_kb, g_subln, w_o, g_ffn, w_router, b_router, w_gu, b_gu, w_down, b_down):
    b, s, d = x.shape
    n = b * s
    k_sel = min(TOPK_KEYS, s // 4)
    c_tab, s_tab = _rope_tables(positions)
    for layer in range(g_attn.shape[0]):
        lam_init = 0.8 - 0.6 * math.exp(-0.3 * layer)
        x2 = x.reshape(n, d)
        aq, iq, bq, bk, bv, kka, kki, v_t, iw_t = _in_proj(
            x2, g_attn[layer].reshape(1, d), w_in[layer], c_tab, s_tab,
            g_qa[layer], g_ka[layer], g_idx_k[layer], g_qb[layer], g_kb[layer])
        seq = lambda t: t.reshape(b, s, t.shape[-1])
        y_a = _dsa_attn(seq(aq), seq(iq), seq(kka), seq(kki), v_t, iw_t, k_sel)
        lam = (jnp.exp(jnp.sum(lambda_q1[layer].astype(F32) * lambda_k1[layer].astype(F32)))
               - jnp.exp(jnp.sum(lambda_q2[layer].astype(F32) * lambda_k2[layer].astype(F32)))
               + lam_init).reshape(1)
        y_b = _diff_attn(seq(bq), seq(bk), seq(bv), lam, g_subln[layer].reshape(1, B_V_DIM), lam_init)
        x1, hp, top_idx, gates, counts = _out_proj(
            x2, y_a.reshape(n, -1), y_b.reshape(n, -1), w_o[layer], g_ffn[layer].reshape(1, d),
            w_router[layer], b_router[layer])
        dest, block_expert, n_used, n_blocks = _moe_plan(
            top_idx, counts[:, 0, :N_EXPERTS], n // counts.shape[0])
        xs = _moe_dispatch(hp, dest, n_blocks * MOE_ROWS)
        y_rows = _moe_experts(xs, block_expert, n_used, w_gu[layer], b_gu[layer], w_down[layer], b_down[layer])
        x = _moe_combine(x1, y_rows, dest, gates).reshape(b, s, d)
    return x
```

```python
import functools
import math

import jax
import jax.numpy as jnp
import numpy as np
from jax import lax
from jax.experimental import pallas as pl
from jax.experimental.pallas import tpu as pltpu

CHUNK = 64
ROPE_THETA = 500000.0
NORM_EPS = 1e-6
A_HEADS = 8
HEAD_DIM = 64
IDX_HEADS = 8
TOPK_KEYS = 256
B_HEADS = 4
B_V_DIM = 128
ROT_DIM = HEAD_DIM // 4
ROT_HALF = ROT_DIM // 2
N_EXPERTS = 32
TOP_EXPERTS = 4
SWIGLU_LIMIT = 7.0
SWIGLU_ALPHA = 1.702

LANES = 128
VMEM_LIMIT = 48 * 1024 * 1024
MOE_VMEM_LIMIT = 56 * 1024 * 1024

PROJ_ROWS = 512
DIFF_Q = 256
DIFF_K = 512
DSA_Q = 128
DSA_K = 256
MOE_ROWS = 512
COMBINE_ROWS = 256
ISSUE_GROUP = 4

MASKED = -1e30
INT_MIN = -(2 ** 31)

BF16 = jnp.bfloat16
F32 = jnp.float32
NT_DIMS = (((1,), (1,)), ((), ()))


def _params(*sem):
    return pltpu.CompilerParams(dimension_semantics=sem, vmem_limit_bytes=VMEM_LIMIT)


def _rope_kernel(invf_ref, pos_ref, cos_ref, sin_ref):
    f = pl.program_id(0)
    ang = pos_ref[...].astype(F32) * invf_ref[f]
    cos_ref[0] = jnp.cos(ang)
    sin_ref[0] = jnp.sin(ang)


def _rope_tables(positions):
    b, s = positions.shape
    inv_freq = ROPE_THETA ** (-jnp.arange(0, ROT_DIM, 2, dtype=F32) / ROT_DIM)
    cos_t, sin_t = pl.pallas_call(
        _rope_kernel,
        out_shape=(jax.ShapeDtypeStruct((ROT_HALF, b, s), F32),) * 2,
        grid=(ROT_HALF,),
        in_specs=[pl.BlockSpec(memory_space=pltpu.SMEM),
                  pl.BlockSpec((b, s), lambda f: (0, 0))],
        out_specs=(pl.BlockSpec((1, b, s), lambda f: (f, 0, 0)),) * 2,
        compiler_params=_params("arbitrary"),
        name="rope_tables",
    )(inv_freq, positions)
    n = b * s
    cos8 = jnp.transpose(cos_t, (1, 2, 0)).reshape(n, ROT_HALF)
    sin8 = jnp.transpose(sin_t, (1, 2, 0)).reshape(n, ROT_HALF)
    ones = jnp.ones((n, HEAD_DIM - ROT_DIM), F32)
    zeros = jnp.zeros((n, HEAD_DIM - ROT_DIM), F32)
    c_tab = jnp.concatenate([cos8, cos8, ones], axis=1)
    s_tab = jnp.concatenate([-sin8, sin8, zeros], axis=1)
    return jnp.tile(c_tab, (1, 2)), jnp.tile(s_tab, (1, 2))


def _segment_rms(y, gain):
    lane = lax.broadcasted_iota(jnp.int32, (1, LANES), 1)
    lo = lane < HEAD_DIM
    y2 = y * y
    s0 = jnp.sum(jnp.where(lo, y2, 0.0), axis=-1, keepdims=True)
    s1 = jnp.sum(jnp.where(lo, 0.0, y2), axis=-1, keepdims=True)
    ms = jnp.where(lo, s0, s1) * (1.0 / HEAD_DIM)
    return y * lax.rsqrt(ms + NORM_EPS) * gain


def _rope128(y, c_tab, s_tab):
    lane = lax.broadcasted_iota(jnp.int32, (1, LANES), 1)
    first = (lane % HEAD_DIM) < ROT_HALF
    up = pltpu.roll(y, LANES - ROT_HALF, 1)
    down = pltpu.roll(y, ROT_HALF, 1)
    return y * c_tab + jnp.where(first, up, down) * s_tab


def _in_proj_kernel(x_ref, g_ref, w_ref, c_ref, s_ref, gains_ref,
                    aq_ref, iq_ref, bq_ref, bk_ref, bv_ref, kka_ref, kki_ref, vt_ref, iw_ref):
    x = x_ref[...]
    ms = jnp.mean(x * x, axis=-1, keepdims=True)
    h = (x * lax.rsqrt(ms + NORM_EPS) * g_ref[...]).astype(BF16)
    c_tab = c_ref[...]
    s_tab = s_ref[...]
    wide = 4 * LANES

    def group(col, out_ref, gain_row, rope):
        acc = jnp.dot(h, w_ref[:, col:col + wide], preferred_element_type=F32)
        for t in range(4):
            y = acc[:, t * LANES:(t + 1) * LANES]
            if gain_row is not None:
                y = _segment_rms(y, gains_ref[gain_row:gain_row + 1, :])
            if rope:
                y = _rope128(y, c_tab, s_tab)
            out_ref[:, t * LANES:(t + 1) * LANES] = y.astype(out_ref.dtype)

    group(0 * wide, aq_ref, 0, True)
    group(1 * wide, iq_ref, None, True)
    group(2 * wide, bq_ref, 1, True)
    group(3 * wide, bk_ref, 2, True)
    group(4 * wide, bv_ref, None, False)
    acc = jnp.dot(h, w_ref[:, 5 * wide:6 * wide], preferred_element_type=F32)
    ka = _rope128(_segment_rms(acc[:, 0:LANES], gains_ref[3:4, :]), c_tab, s_tab)
    ki = _rope128(_segment_rms(acc[:, LANES:2 * LANES], gains_ref[4:5, :]), c_tab, s_tab)
    kka_ref[...] = ka.astype(BF16)
    kki_ref[...] = ki.astype(BF16)
    tk = vt_ref.shape[2]
    for c in range(vt_ref.shape[0]):
        v_t = acc[c * tk:(c + 1) * tk, 2 * LANES:3 * LANES].T
        vt_ref[c] = v_t[:HEAD_DIM].astype(BF16)
    iw_t = (acc[:, 3 * LANES:4 * LANES] * ((IDX_HEADS * HEAD_DIM) ** -0.5)).T
    iw_ref[...] = iw_t[:IDX_HEADS]


def _in_proj(x2, g_attn, w_in, c_tab, s_tab, g_qa, g_ka, g_idx_k, g_qb, g_kb):
    n, d = x2.shape
    ha = A_HEADS * HEAD_DIM
    o = np.cumsum([0, ha, HEAD_DIM, HEAD_DIM, IDX_HEADS * HEAD_DIM, HEAD_DIM, IDX_HEADS,
                   B_HEADS * 2 * HEAD_DIM, B_HEADS * 2 * HEAD_DIM, B_HEADS * B_V_DIM])
    aq, ak, av, iq, ik, iw, bq, bk, bv = [w_in[:, o[i]:o[i + 1]] for i in range(9)]
    pad = jnp.zeros((d, LANES - IDX_HEADS), w_in.dtype)
    w = jnp.concatenate([aq, iq, bq, bk, bv, ak, ak, ik, ik, av, av, iw, pad], axis=1).astype(BF16)
    two = lambda g: jnp.tile(g.reshape(1, HEAD_DIM), (1, 2))
    gains = jnp.concatenate([two(g_qa), two(g_qb), two(g_kb), two(g_ka), two(g_idx_k),
                             jnp.zeros((3, LANES), F32)], axis=0)
    tm = min(PROJ_ROWS, n)
    tk = min(DSA_K, tm)
    wide = 4 * LANES
    row = lambda width: pl.BlockSpec((tm, width), lambda i: (i, 0))
    full = lambda a: pl.BlockSpec(a.shape, lambda i: (0, 0))
    outs = pl.pallas_call(
        _in_proj_kernel,
        out_shape=(jax.ShapeDtypeStruct((n, wide), BF16),) * 5
        + (jax.ShapeDtypeStruct((n, LANES), BF16),) * 2
        + (jax.ShapeDtypeStruct((n // tk, HEAD_DIM, tk), BF16),
           jax.ShapeDtypeStruct((IDX_HEADS, n), F32)),
        grid=(n // tm,),
        in_specs=[row(d), full(g_attn), full(w), row(LANES), row(LANES), full(gains)],
        out_specs=(row(wide),) * 5 + (row(LANES),) * 2
        + (pl.BlockSpec((tm // tk, HEAD_DIM, tk), lambda i: (i, 0, 0)),
           pl.BlockSpec((IDX_HEADS, tm), lambda i: (0, i))),
        compiler_params=_params("arbitrary"),
        name="in_proj",
    )(x2, g_attn, w, c_tab, s_tab, gains)
    return outs


def _diff_attn_kernel(lam_ref, q_ref, k_ref, v_ref, g_ref, o_ref, vt_ref, *, lam_init):
    i = pl.program_id(1)
    tq = q_ref.shape[1]
    tk = vt_ref.shape[3]
    lam = lam_ref[0]
    nkb = (i * tq + tq + tk - 1) // tk
    lane = lax.broadcasted_iota(jnp.int32, (1, LANES), 1)
    lo = lane < HEAD_DIM
    k_chunk0 = lax.broadcasted_iota(jnp.int32, (tk, tq), 0) // CHUNK
    q_chunk = (i * tq + lax.broadcasted_iota(jnp.int32, (tk, tq), 1)) // CHUNK

    @pl.when(i == 0)
    def _():
        for hd in range(B_HEADS):
            for c in range(vt_ref.shape[1]):
                blk = v_ref[0, c * tk:(c + 1) * tk, hd * LANES:(hd + 1) * LANES]
                vt_ref[hd, c] = blk.astype(F32).T.astype(BF16)

    qs = []
    for hd in range(B_HEADS):
        qh = q_ref[0, :, hd * LANES:(hd + 1) * LANES] * jnp.asarray(HEAD_DIM ** -0.5, BF16)
        zero = jnp.zeros_like(qh)
        qs.append(jnp.concatenate([jnp.where(lo, qh, zero), jnp.where(lo, zero, qh)], axis=0))

    def body(j, carry):
        start = pl.multiple_of(j * tk, tk)
        ok = k_chunk0 + j * (tk // CHUNK) <= q_chunk
        out = []
        for hd in range(B_HEADS):
            m, l, acc = carry[hd]
            kj = k_ref[0, pl.ds(start, tk), hd * LANES:(hd + 1) * LANES]
            st = lax.dot_general(kj, qs[hd], NT_DIMS, preferred_element_type=F32)
            s = jnp.concatenate([jnp.where(ok, st[:, :tq], MASKED), jnp.where(ok, st[:, tq:], MASKED)],
                                axis=1)
            m_new = jnp.maximum(m, jnp.max(s, axis=0, keepdims=True))
            alpha = jnp.exp(m - m_new)
            p = jnp.exp(s - m_new)
            l = alpha * l + jnp.sum(p, axis=0, keepdims=True)
            acc = alpha * acc + jnp.dot(vt_ref[hd, j], p.astype(BF16), preferred_element_type=F32)
            out.append((m_new, l, acc))
        return tuple(out)

    init = tuple((jnp.full((1, 2 * tq), MASKED, F32), jnp.zeros((1, 2 * tq), F32),
                  jnp.zeros((B_V_DIM, 2 * tq), F32)) for _ in range(B_HEADS))
    final = lax.fori_loop(0, nkb, body, init)
    for hd in range(B_HEADS):
        cols = slice(hd * LANES, (hd + 1) * LANES)
        _, l, acc = final[hd]
        o = acc / l
        o = o[:, :tq] - lam * o[:, tq:]
        ms = jnp.mean(o * o, axis=0, keepdims=True)
        o = o * lax.rsqrt(ms + NORM_EPS) * g_ref[...] * (1.0 - lam_init)
        o_ref[0, :, cols] = o.T.astype(o_ref.dtype)


def _diff_attn(bq, bk, bv, lam, g_subln, lam_init):
    b, s, w = bq.shape
    tq = min(DIFF_Q, s)
    tk = min(DIFF_K, s)
    kv = pl.BlockSpec((1, s, w), lambda bi, i: (bi, 0, 0))
    qo = pl.BlockSpec((1, tq, w), lambda bi, i: (bi, i, 0))
    return pl.pallas_call(
        functools.partial(_diff_attn_kernel, lam_init=lam_init),
        out_shape=jax.ShapeDtypeStruct((b, s, w), BF16),
        grid=(b, s // tq),
        in_specs=[pl.BlockSpec(memory_space=pltpu.SMEM), qo, kv, kv,
                  pl.BlockSpec((B_V_DIM, 1), lambda bi, i: (0, 0))],
        out_specs=qo,
        scratch_shapes=[pltpu.VMEM((B_HEADS, s // tk, B_V_DIM, tk), BF16)],
        compiler_params=_params("arbitrary", "arbitrary"),
        name="diff_attn",
    )(lam, bq, bk, bv, g_subln.reshape(B_V_DIM, 1))


def _stack_heads(src_ref, dst_ref, scale):
    tq = src_ref.shape[1]
    lane = lax.broadcasted_iota(jnp.int32, (1, LANES), 1)
    lo = lane < HEAD_DIM
    for hd in range(A_HEADS):
        blk = src_ref[0, :, (hd // 2) * LANES:(hd // 2 + 1) * LANES]
        if scale is not None:
            blk = blk * jnp.asarray(scale, blk.dtype)
        keep = lo if hd % 2 == 0 else jnp.logical_not(lo)
        dst_ref[hd * tq:(hd + 1) * tq, :] = jnp.where(keep, blk, jnp.zeros_like(blk))


def _dsa_attn_kernel(aq_ref, iq_ref, kka_ref, kki_ref, vt_ref, iw_ref, o_ref,
                     qa_ref, qi_ref, key_ref, *, k_sel):
    i = pl.program_id(1)
    tq = aq_ref.shape[1]
    tk = key_ref.shape[1]
    nsel = (i * tq + tq + tk - 1) // tk
    _stack_heads(aq_ref, qa_ref, HEAD_DIM ** -0.5)
    _stack_heads(iq_ref, qi_ref, None)
    sub = lax.broadcasted_iota(jnp.int32, (tk, tq), 0)
    k_chunk0 = sub // CHUNK
    q_chunk = (i * tq + lax.broadcasted_iota(jnp.int32, (tk, tq), 1)) // CHUNK

    def score_block(j, _):
        start = pl.multiple_of(j * tk, tk)
        kj = kki_ref[0, pl.ds(start, tk), :]
        st = lax.dot_general(kj, qi_ref[...], NT_DIMS, preferred_element_type=F32)
        score = jnp.zeros((tk, tq), F32)
        for hd in range(IDX_HEADS):
            rel = jnp.maximum(st[:, hd * tq:(hd + 1) * tq], 0.0)
            score = score + rel * iw_ref[hd:hd + 1, :]
        bits = pltpu.bitcast(score, jnp.int32)
        key = jnp.where(bits < 0, jnp.int32(INT_MIN) - bits, bits)
        key_ref[j] = jnp.where(k_chunk0 + j * (tk // CHUNK) > q_chunk, jnp.int32(INT_MIN), key)
        return 0

    lax.fori_loop(0, nsel, score_block, 0)

    def count(pred):
        def body(j, acc):
            hit = jnp.where(pred(key_ref[j], j), 1, 0).astype(jnp.int32)
            return acc + jnp.sum(hit.reshape(tk // 8, 8, tq), axis=0)
        acc = lax.fori_loop(0, nsel, body, jnp.zeros((8, tq), jnp.int32))
        return jnp.sum(acc, axis=0, keepdims=True)

    def search(groups, start):
        if groups * tk <= k_sel:
            return start

        def search_bit(t, thr):
            cand = thr + lax.shift_left(jnp.int32(1), 31 - t)
            acc = jnp.zeros((8, tq), jnp.int32)
            for j in range(groups):
                hit = jnp.where(key_ref[j] >= cand, 1, 0).astype(jnp.int32)
                acc = acc + jnp.sum(hit.reshape(tk // 8, 8, tq), axis=0)
            cnt = jnp.sum(acc, axis=0, keepdims=True)
            return jnp.where(cnt >= k_sel, cand, thr)
        return lax.fori_loop(0, 32, search_bit, start)

    thr = lax.switch(nsel - 1, [functools.partial(search, g + 1) for g in range(key_ref.shape[0])],
                     jnp.full((1, tq), INT_MIN, jnp.int32))
    n_gt = count(lambda key, j: key > thr)
    n_eq = count(lambda key, j: key == thr)
    need = k_sel - n_gt
    tie_break = jnp.logical_and(thr > INT_MIN, n_eq > need)

    def tie_search(_):
        def bit(t, cut):
            cand = cut + lax.shift_left(jnp.int32(1), 15 - t)
            below = count(lambda key, j: jnp.logical_and(key == thr, sub + j * tk < cand))
            return jnp.where(below < need, cand, cut)
        return lax.fori_loop(0, 16, bit, jnp.zeros((1, tq), jnp.int32))

    any_tie = jnp.max(jnp.where(tie_break, 1, 0)) > 0
    cut = lax.cond(any_tie, tie_search, lambda _: jnp.full((1, tq), 2 ** 30, jnp.int32), 0)
    cut = jnp.where(tie_break, cut, 2 ** 30)

    parts = 1
    part_heads = A_HEADS // parts
    part_rows = part_heads * tq

    def attn_block(j, carry):
        start = pl.multiple_of(j * tk, tk)
        kj = kka_ref[0, pl.ds(start, tk), :]
        key = key_ref[j]
        sel = jnp.logical_or(key > thr, jnp.logical_and(key == thr, sub + j * tk <= cut))
        sel = jnp.logical_and(sel, key > INT_MIN)
        out = []
        for part in range(parts):
            m, l, acc = carry[part]
            q_part = qa_ref[part * part_rows:(part + 1) * part_rows, :]
            st = lax.dot_general(kj, q_part, NT_DIMS, preferred_element_type=F32)
            s = jnp.concatenate([jnp.where(sel, st[:, hd * tq:(hd + 1) * tq], MASKED)
                                 for hd in range(part_heads)], axis=1)
            m_new = jnp.maximum(m, jnp.max(s, axis=0, keepdims=True))
            alpha = jnp.exp(m - m_new)
            p = jnp.exp(s - m_new)
            l = alpha * l + jnp.sum(p, axis=0, keepdims=True)
            acc = alpha * acc + jnp.dot(vt_ref[j], p.astype(BF16), preferred_element_type=F32)
            out.append((m_new, l, acc))
        return tuple(out)

    init = tuple((jnp.full((1, part_rows), MASKED, F32), jnp.zeros((1, part_rows), F32),
                  jnp.zeros((HEAD_DIM, part_rows), F32)) for _ in range(parts))
    final = lax.fori_loop(0, nsel, attn_block, init)
    out_t = jnp.concatenate([acc / l for _, l, acc in final], axis=1)
    for g in range(A_HEADS // 2):
        pair = out_t[:, 2 * g * tq:(2 * g + 2) * tq]
        pair = jnp.concatenate([pair[:, :tq], pair[:, tq:]], axis=0)
        o_ref[0, :, g * LANES:(g + 1) * LANES] = pair.T.astype(o_ref.dtype)


def _dsa_attn(aq, iq, kka, kki, v_t, iw_t, k_sel):
    b, s, w = aq.shape
    tq = min(DSA_Q, s)
    tk = v_t.shape[2]
    nq = s // tq
    qo = pl.BlockSpec((1, tq, w), lambda bi, i: (bi, i, 0))
    kv = pl.BlockSpec((1, s, LANES), lambda bi, i: (bi, 0, 0))
    return pl.pallas_call(
        functools.partial(_dsa_attn_kernel, k_sel=k_sel),
        out_shape=jax.ShapeDtypeStruct((b, s, w), BF16),
        grid=(b, nq),
        in_specs=[qo, qo, kv, kv,
                  pl.BlockSpec((s // tk, HEAD_DIM, tk), lambda bi, i: (bi, 0, 0)),
                  pl.BlockSpec((IDX_HEADS, tq), lambda bi, i: (0, bi * nq + i))],
        out_specs=qo,
        scratch_shapes=[pltpu.VMEM((A_HEADS * tq, LANES), BF16),
                        pltpu.VMEM((IDX_HEADS * tq, LANES), BF16),
                        pltpu.VMEM((s // tk, tk, tq), jnp.int32)],
        compiler_params=_params("arbitrary", "arbitrary"),
        name="dsa_attn",
    )(aq, iq, kka, kki, v_t, iw_t)


def _pack_halves(y):
    w = y.shape[1] // 2
    r = y.astype(BF16).astype(F32)
    lo = pltpu.bitcast(r[:, :w], jnp.uint32)
    hi = pltpu.bitcast(r[:, w:], jnp.uint32)
    return jnp.bitwise_or(jnp.bitwise_and(hi, jnp.uint32(0xFFFF0000)), lax.shift_right_logical(lo, jnp.uint32(16)))


def _unpack_halves(u):
    lo = pltpu.bitcast(lax.shift_left(u, jnp.uint32(16)), F32)
    hi = pltpu.bitcast(jnp.bitwise_and(u, jnp.uint32(0xFFFF0000)), F32)
    return lo, hi


SLAB = 4


def _store_slab(ref, first, packed):
    rows = packed.shape[0]
    for c in range(SLAB):
        ref[pl.ds(SLAB * first + c, rows, stride=SLAB), :] = packed[:, c * LANES:(c + 1) * LANES]


def _load_slab(ref, first, rows):
    return jnp.concatenate([ref[pl.ds(SLAB * first + c, rows, stride=SLAB), :] for c in range(SLAB)], axis=1)


def _out_proj_kernel(x_ref, ya_ref, yb_ref, wa_ref, wb_ref, g_ref, wr_ref, br_ref,
                     x1_ref, hp_ref, idx_ref, gate_ref, cnt_ref):
    x1 = (x_ref[...] + jnp.dot(ya_ref[...], wa_ref[...], preferred_element_type=F32)
          + jnp.dot(yb_ref[...], wb_ref[...], preferred_element_type=F32))
    x1_ref[...] = x1
    ms = jnp.mean(x1 * x1, axis=-1, keepdims=True)
    h = x1 * lax.rsqrt(ms + NORM_EPS) * g_ref[...]
    _store_slab(hp_ref, 0, _pack_halves(h))
    logits = jnp.dot(h.astype(BF16), wr_ref[...], preferred_element_type=F32) + br_ref[...]
    tm = logits.shape[0]
    lane = lax.broadcasted_iota(jnp.int32, logits.shape, 1)
    work = jnp.where(lane < N_EXPERTS, logits, -jnp.inf)
    val_out = jnp.full(logits.shape, -jnp.inf, F32)
    member = jnp.zeros(logits.shape, F32)
    args = []
    for r in range(TOP_EXPERTS):
        top = jnp.max(work, axis=-1, keepdims=True)
        arg = jnp.min(jnp.where(work == top, lane, LANES), axis=-1, keepdims=True)
        args.append(arg)
        val_out = jnp.where(lane == r, top, val_out)
        member = jnp.where(lane == arg, 1.0, member)
        work = jnp.where(lane == arg, -jnp.inf, work)
    e = jnp.exp(val_out - jnp.max(val_out, axis=-1, keepdims=True))
    gate_ref[...] = e / jnp.sum(e, axis=-1, keepdims=True)
    earlier = jnp.where(lax.broadcasted_iota(jnp.int32, (tm, tm), 1)
                        < lax.broadcasted_iota(jnp.int32, (tm, tm), 0), 1.0, 0.0).astype(BF16)
    before = jnp.dot(earlier, member.astype(BF16), preferred_element_type=F32)
    idx_out = jnp.zeros(logits.shape, jnp.int32)
    for r in range(TOP_EXPERTS):
        rank = jnp.sum(jnp.where(lane == args[r], before, 0.0), axis=-1, keepdims=True)
        idx_out = jnp.where(lane == r, args[r], idx_out)
        idx_out = jnp.where(lane == TOP_EXPERTS + r, rank.astype(jnp.int32), idx_out)
    idx_ref[...] = idx_out
    counts = jnp.sum(member, axis=0, keepdims=True).astype(jnp.int32)
    cnt_ref[0] = jnp.broadcast_to(counts, (8, LANES))


def _out_proj(x2, y_a, y_b, w_o, g_ffn, w_router, b_router):
    n, d = x2.shape
    da = y_a.shape[1]
    wa = w_o[:da].astype(BF16)
    wb = w_o[da:].astype(BF16)
    wr = jnp.concatenate([w_router, jnp.zeros((d, LANES - N_EXPERTS), w_router.dtype)], 1).astype(BF16)
    br = jnp.concatenate([b_router, jnp.zeros((LANES - N_EXPERTS,), F32)]).reshape(1, LANES)
    tm = min(PROJ_ROWS, n)
    row = lambda width: pl.BlockSpec((tm, width), lambda i: (i, 0))
    full = lambda a: pl.BlockSpec(a.shape, lambda i: (0, 0))
    return pl.pallas_call(
        _out_proj_kernel,
        out_shape=(jax.ShapeDtypeStruct((n, d), F32), jax.ShapeDtypeStruct((SLAB * n, LANES), jnp.uint32),
                   jax.ShapeDtypeStruct((n, LANES), jnp.int32), jax.ShapeDtypeStruct((n, LANES), F32),
                   jax.ShapeDtypeStruct((n // tm, 8, LANES), jnp.int32)),
        grid=(n // tm,),
        in_specs=[row(d), row(da), row(y_b.shape[1]), full(wa), full(wb), full(g_ffn), full(wr), full(br)],
        out_specs=(row(d), pl.BlockSpec((SLAB * tm, LANES), lambda i: (i, 0)), row(LANES), row(LANES),
                   pl.BlockSpec((1, 8, LANES), lambda i: (i, 0, 0))),
        compiler_params=_params("arbitrary"),
        name="out_proj",
    )(x2, y_a, y_b, wa, wb, g_ffn, wr, br)


def _moe_plan(top_idx, counts, tile_rows):
    n = top_idx.shape[0]
    rows = MOE_ROWS
    n_blocks = -(-(n * TOP_EXPERTS + N_EXPERTS * (rows - 1)) // rows)
    total = jnp.sum(counts, axis=0)
    padded = ((total + rows - 1) // rows) * rows
    pend = jnp.cumsum(padded)
    base = (pend - padded)[None, :] + jnp.cumsum(counts, axis=0) - counts
    expert = top_idx[:, :TOP_EXPERTS]
    rank = top_idx[:, TOP_EXPERTS:2 * TOP_EXPERTS]
    base_tok = jnp.repeat(base, tile_rows, axis=0)
    hit = expert[:, :, None] == jnp.arange(N_EXPERTS, dtype=jnp.int32)[None, None, :]
    dest = jnp.sum(jnp.where(hit, base_tok[:, None, :], 0), axis=-1) + rank
    block_start = jnp.arange(n_blocks, dtype=jnp.int32) * rows
    block_expert = jnp.minimum(jnp.sum((block_start[:, None] >= pend[None, :]).astype(jnp.int32), axis=1),
                               N_EXPERTS - 1).astype(jnp.int32)
    n_used = (pend[-1] // rows).astype(jnp.int32).reshape(1)
    return dest.astype(jnp.int32).reshape(-1), block_expert, n_used, n_blocks


def _dispatch_kernel(dest_ref, h_ref, xs_in, xs_hbm, sem):
    del xs_in
    per_step = dest_ref.shape[2]
    tm = per_step // TOP_EXPERTS

    def issue(g, _):
        first = g * (ISSUE_GROUP * TOP_EXPERTS)
        slots = [dest_ref[0, 0, first + j] for j in range(ISSUE_GROUP * TOP_EXPERTS)]
        for j, slot in enumerate(slots):
            t = g * ISSUE_GROUP + j // TOP_EXPERTS
            src = h_ref.at[pl.ds(pl.multiple_of(SLAB * t, SLAB), SLAB)]
            dst = xs_hbm.at[pl.ds(pl.multiple_of(SLAB * slot, SLAB), SLAB)]
            pltpu.make_async_copy(src, dst, sem).start(priority=j % 2)
        return 0

    lax.fori_loop(0, tm // ISSUE_GROUP, issue, 0)
    whole = xs_hbm.at[pl.ds(0, SLAB * per_step)]
    pltpu.make_async_copy(whole, whole, sem).wait()


def _moe_dispatch(hp, dest, m):
    n = hp.shape[0] // SLAB
    tm = min(PROJ_ROWS, n)
    per_step = tm * TOP_EXPERTS
    return pl.pallas_call(
        _dispatch_kernel,
        out_shape=jax.ShapeDtypeStruct((SLAB * m, LANES), jnp.uint32),
        grid=(n // tm,),
        in_specs=[pl.BlockSpec((1, 1, per_step), lambda i: (i, 0, 0), memory_space=pltpu.SMEM),
                  pl.BlockSpec((SLAB * tm, LANES), lambda i: (i, 0)), pl.BlockSpec(memory_space=pl.ANY)],
        out_specs=pl.BlockSpec(memory_space=pl.ANY),
        scratch_shapes=[pltpu.SemaphoreType.DMA],
        input_output_aliases={2: 0},
        compiler_params=_params("arbitrary"),
        name="moe_dispatch",
    )(dest.reshape(n // tm, 1, per_step), hp, jnp.zeros((SLAB * m, LANES), jnp.uint32))


def _moe_kernel(bexp_ref, nused_ref, x_ref, wgu_ref, bgu_ref, wd_ref, bd_ref, y_ref, wgu_bf, wd_bf):
    i = pl.program_id(0)
    rows = x_ref.shape[0] // SLAB
    half = SLAB * LANES
    f = wd_ref.shape[1]

    @pl.when(jnp.logical_or(i == 0, bexp_ref[i] != bexp_ref[jnp.maximum(i - 1, 0)]))
    def _():
        wgu_bf[...] = wgu_ref[0].astype(BF16)
        wd_bf[...] = wd_ref[0].astype(BF16)

    @pl.when(i < nused_ref[0])
    def _():
        lo, hi = _unpack_halves(_load_slab(x_ref, 0, rows))
        gu = (jnp.dot(lo.astype(BF16), wgu_bf[:half, :], preferred_element_type=F32)
              + jnp.dot(hi.astype(BF16), wgu_bf[half:, :], preferred_element_type=F32)
              + bgu_ref[0])
        glu = jnp.minimum(gu[:, :f], SWIGLU_LIMIT)
        lin = jnp.clip(gu[:, f:], -SWIGLU_LIMIT, SWIGLU_LIMIT)
        act = glu * jax.nn.sigmoid(SWIGLU_ALPHA * glu) * (lin + 1.0)
        y = jnp.dot(act.astype(BF16), wd_bf[...], preferred_element_type=F32) + bd_ref[0]
        _store_slab(y_ref, 0, _pack_halves(y))

    @pl.when(i >= nused_ref[0])
    def _():
        y_ref[...] = jnp.zeros(y_ref.shape, y_ref.dtype)


def _moe_experts(xs, block_expert, n_used, w_gu, b_gu, w_down, b_down):
    m = xs.shape[0] // SLAB
    d = 2 * SLAB * LANES
    f = w_down.shape[1]
    rows = MOE_ROWS
    bgu = b_gu.reshape(N_EXPERTS, 1, 2 * f)
    bd = b_down.reshape(N_EXPERTS, 1, d)
    slots = pl.BlockSpec((SLAB * rows, LANES), lambda i, be, nu: (i, 0))
    by_expert = lambda shape: pl.BlockSpec((1,) + shape, lambda i, be, nu: (be[i], 0, 0))
    return pl.pallas_call(
        _moe_kernel,
        out_shape=jax.ShapeDtypeStruct((SLAB * m, LANES), jnp.uint32),
        grid_spec=pltpu.PrefetchScalarGridSpec(
            num_scalar_prefetch=2,
            grid=(m // rows,),
            in_specs=[slots, by_expert((d, 2 * f)), by_expert((1, 2 * f)),
                      by_expert((f, d)), by_expert((1, d))],
            out_specs=slots,
            scratch_shapes=[pltpu.VMEM((d, 2 * f), BF16), pltpu.VMEM((f, d), BF16)]),
        compiler_params=pltpu.CompilerParams(dimension_semantics=("arbitrary",),
                                             vmem_limit_bytes=MOE_VMEM_LIMIT),
        name="moe_experts",
    )(block_expert, n_used, xs, w_gu, bgu, w_down, bd)


def _combine_kernel(dcur_ref, dnext_ref, y_hbm, x_ref, gate_ref, o_ref, ybuf, sems):
    i = pl.program_id(0)
    nsteps = pl.num_programs(0)
    tm = x_ref.shape[0]
    half = SLAB * LANES
    slot = lax.rem(i, 2)

    def issue(dest_ref, buf):
        def body(g, _):
            first = g * (ISSUE_GROUP * TOP_EXPERTS)
            slots = [dest_ref[0, 0, first + j] for j in range(ISSUE_GROUP * TOP_EXPERTS)]
            for j, slot in enumerate(slots):
                t = g * ISSUE_GROUP + j // TOP_EXPERTS
                src = pl.multiple_of(SLAB * slot, SLAB)
                dst = pl.multiple_of(SLAB * ((j % TOP_EXPERTS) * tm + t), SLAB)
                pltpu.make_async_copy(y_hbm.at[pl.ds(src, SLAB)], ybuf.at[buf, pl.ds(dst, SLAB)],
                                      sems.at[buf]).start(priority=j % 2)
            return 0
        lax.fori_loop(0, tm // ISSUE_GROUP, body, 0)

    @pl.when(i == 0)
    def _():
        issue(dcur_ref, 0)

    @pl.when(i + 1 < nsteps)
    def _():
        issue(dnext_ref, 1 - slot)

    pltpu.make_async_copy(y_hbm.at[pl.ds(0, SLAB * TOP_EXPERTS * tm)], ybuf.at[slot], sems.at[slot]).wait()
    lo = x_ref[:, :half]
    hi = x_ref[:, half:]
    for k in range(TOP_EXPERTS):
        a, b = _unpack_halves(_load_slab(ybuf.at[slot], k * tm, tm))
        g = gate_ref[:, k:k + 1]
        lo = lo + a * g
        hi = hi + b * g
    o_ref[:, :half] = lo
    o_ref[:, half:] = hi


def _moe_combine(x1, y_rows, dest, gates):
    n, d = x1.shape
    tm = min(COMBINE_ROWS, n)
    nt = n // tm
    per_step = tm * TOP_EXPERTS
    dest3 = dest.reshape(nt, 1, per_step)
    return pl.pallas_call(
        _combine_kernel,
        out_shape=jax.ShapeDtypeStruct((n, d), F32),
        grid=(nt,),
        in_specs=[pl.BlockSpec((1, 1, per_step), lambda i: (i, 0, 0), memory_space=pltpu.SMEM),
                  pl.BlockSpec((1, 1, per_step), lambda i: (jnp.minimum(i + 1, nt - 1), 0, 0),
                               memory_space=pltpu.SMEM),
                  pl.BlockSpec(memory_space=pl.ANY),
                  pl.BlockSpec((tm, d), lambda i: (i, 0)),
                  pl.BlockSpec((tm, LANES), lambda i: (i, 0))],
        out_specs=pl.BlockSpec((tm, d), lambda i: (i, 0)),
        scratch_shapes=[pltpu.VMEM((2, SLAB * per_step, LANES), jnp.uint32), pltpu.SemaphoreType.DMA((2,))],
        compiler_params=_params("arbitrary"),
        name="moe_combine",
    )(dest3, dest3, y_rows, x1, gates)


def kernel(x, positions, g_attn, w_in, g_qa, g_ka, g_idx_k, lambda_q1, lambda_k1, lambda_q2, lambda_k2, g_qb, g_kb, g_subln, w_o, g_ffn, w_router, b_router, w_gu, b_gu, w_down, b_down):
    b, s, d = x.shape
    n = b * s
    k_sel = min(TOPK_KEYS, s // 4)
    c_tab, s_tab = _rope_tables(positions)
    for layer in range(g_attn.shape[0]):
        lam_init = 0.8 - 0.6 * math.exp(-0.3 * layer)
        x2 = x.reshape(n, d)
        aq, iq, bq, bk, bv, kka, kki, v_t, iw_t = _in_proj(
            x2, g_attn[layer].reshape(1, d), w_in[layer], c_tab, s_tab,
            g_qa[layer], g_ka[layer], g_idx_k[layer], g_qb[layer], g_kb[layer])
        seq = lambda t: t.reshape(b, s, t.shape[-1])
        y_a = _dsa_attn(seq(aq), seq(iq), seq(kka), seq(kki), v_t, iw_t, k_sel)
        lam = (jnp.exp(jnp.sum(lambda_q1[layer].astype(F32) * lambda_k1[layer].astype(F32)))
               - jnp.exp(jnp.sum(lambda_q2[layer].astype(F32) * lambda_k2[layer].astype(F32)))
               + lam_init).reshape(1)
        y_b = _diff_attn(seq(bq), seq(bk), seq(bv), lam, g_subln[layer].reshape(1, B_V_DIM), lam_init)
        x1, hp, top_idx, gates, counts = _out_proj(
            x2, y_a.reshape(n, -1), y_b.reshape(n, -1), w_o[layer], g_ffn[layer].reshape(1, d),
            w_router[layer], b_router[layer])
        dest, block_expert, n_used, n_blocks = _moe_plan(
            top_idx, counts[:, 0, :N_EXPERTS], n // counts.shape[0])
        xs = _moe_dispatch(hp, dest, n_blocks * MOE_ROWS)
        y_rows = _moe_experts(xs, block_expert, n_used, w_gu[layer], b_gu[layer], w_down[layer], b_down[layer])
        x = _moe_combine(x1, y_rows, dest, gates).reshape(b, s, d)
    return x
```

```python
import functools
import math

import jax
import jax.numpy as jnp
import numpy as np
from jax import lax
from jax.experimental import pallas as pl
from jax.experimental.pallas import tpu as pltpu

CHUNK = 64
ROPE_THETA = 500000.0
NORM_EPS = 1e-6
A_HEADS = 8
HEAD_DIM = 64
IDX_HEADS = 8
TOPK_KEYS = 256
B_HEADS = 4
B_V_DIM = 128
ROT_DIM = HEAD_DIM // 4
ROT_HALF = ROT_DIM // 2
N_EXPERTS = 32
TOP_EXPERTS = 4
SWIGLU_LIMIT = 7.0
SWIGLU_ALPHA = 1.702

LANES = 128
VMEM_LIMIT = 48 * 1024 * 1024
MOE_VMEM_LIMIT = 56 * 1024 * 1024

PROJ_ROWS = 512
DIFF_Q = 256
DIFF_K = 512
DSA_Q = 256
DSA_K = 256
MOE_ROWS = 512
COMBINE_ROWS = 256
ISSUE_GROUP = 4

MASKED = -1e30
INT_MIN = -(2 ** 31)

BF16 = jnp.bfloat16
F32 = jnp.float32
NT_DIMS = (((1,), (1,)), ((), ()))


def _params(*sem):
    return pltpu.CompilerParams(dimension_semantics=sem, vmem_limit_bytes=VMEM_LIMIT)


def _rope_kernel(invf_ref, pos_ref, cos_ref, sin_ref):
    f = pl.program_id(0)
    ang = pos_ref[...].astype(F32) * invf_ref[f]
    cos_ref[0] = jnp.cos(ang)
    sin_ref[0] = jnp.sin(ang)


def _rope_tables(positions):
    b, s = positions.shape
    inv_freq = ROPE_THETA ** (-jnp.arange(0, ROT_DIM, 2, dtype=F32) / ROT_DIM)
    cos_t, sin_t = pl.pallas_call(
        _rope_kernel,
        out_shape=(jax.ShapeDtypeStruct((ROT_HALF, b, s), F32),) * 2,
        grid=(ROT_HALF,),
        in_specs=[pl.BlockSpec(memory_space=pltpu.SMEM),
                  pl.BlockSpec((b, s), lambda f: (0, 0))],
        out_specs=(pl.BlockSpec((1, b, s), lambda f: (f, 0, 0)),) * 2,
        compiler_params=_params("arbitrary"),
        name="rope_tables",
    )(inv_freq, positions)
    n = b * s
    cos8 = jnp.transpose(cos_t, (1, 2, 0)).reshape(n, ROT_HALF)
    sin8 = jnp.transpose(sin_t, (1, 2, 0)).reshape(n, ROT_HALF)
    ones = jnp.ones((n, HEAD_DIM - ROT_DIM), F32)
    zeros = jnp.zeros((n, HEAD_DIM - ROT_DIM), F32)
    c_tab = jnp.concatenate([cos8, cos8, ones], axis=1)
    s_tab = jnp.concatenate([-sin8, sin8, zeros], axis=1)
    return jnp.tile(c_tab, (1, 2)), jnp.tile(s_tab, (1, 2))


def _segment_rms(y, gain):
    lane = lax.broadcasted_iota(jnp.int32, (1, LANES), 1)
    lo = lane < HEAD_DIM
    y2 = y * y
    s0 = jnp.sum(jnp.where(lo, y2, 0.0), axis=-1, keepdims=True)
    s1 = jnp.sum(jnp.where(lo, 0.0, y2), axis=-1, keepdims=True)
    ms = jnp.where(lo, s0, s1) * (1.0 / HEAD_DIM)
    return y * lax.rsqrt(ms + NORM_EPS) * gain


def _rope128(y, c_tab, s_tab):
    lane = lax.broadcasted_iota(jnp.int32, (1, LANES), 1)
    first = (lane % HEAD_DIM) < ROT_HALF
    up = pltpu.roll(y, LANES - ROT_HALF, 1)
    down = pltpu.roll(y, ROT_HALF, 1)
    return y * c_tab + jnp.where(first, up, down) * s_tab


def _in_proj_kernel(x_ref, g_ref, w_ref, c_ref, s_ref, gains_ref,
                    aq_ref, iq_ref, bq_ref, bk_ref, bv_ref, kka_ref, kki_ref, vt_ref, iw_ref):
    x = x_ref[...]
    ms = jnp.mean(x * x, axis=-1, keepdims=True)
    h = (x * lax.rsqrt(ms + NORM_EPS) * g_ref[...]).astype(BF16)
    c_tab = c_ref[...]
    s_tab = s_ref[...]
    wide = 4 * LANES

    def group(col, out_ref, gain_row, rope):
        acc = jnp.dot(h, w_ref[:, col:col + wide], preferred_element_type=F32)
        for t in range(4):
            y = acc[:, t * LANES:(t + 1) * LANES]
            if gain_row is not None:
                y = _segment_rms(y, gains_ref[gain_row:gain_row + 1, :])
            if rope:
                y = _rope128(y, c_tab, s_tab)
            out_ref[:, t * LANES:(t + 1) * LANES] = y.astype(out_ref.dtype)

    group(0 * wide, aq_ref, 0, True)
    group(1 * wide, iq_ref, None, True)
    group(2 * wide, bq_ref, 1, True)
    group(3 * wide, bk_ref, 2, True)
    group(4 * wide, bv_ref, None, False)
    acc = jnp.dot(h, w_ref[:, 5 * wide:6 * wide], preferred_element_type=F32)
    ka = _rope128(_segment_rms(acc[:, 0:LANES], gains_ref[3:4, :]), c_tab, s_tab)
    ki = _rope128(_segment_rms(acc[:, LANES:2 * LANES], gains_ref[4:5, :]), c_tab, s_tab)
    kka_ref[...] = ka.astype(BF16)
    kki_ref[...] = ki.astype(BF16)
    tk = vt_ref.shape[2]
    for c in range(vt_ref.shape[0]):
        v_t = acc[c * tk:(c + 1) * tk, 2 * LANES:3 * LANES].T
        vt_ref[c] = v_t[:HEAD_DIM].astype(BF16)
    iw_t = (acc[:, 3 * LANES:4 * LANES] * ((IDX_HEADS * HEAD_DIM) ** -0.5)).T
    iw_ref[...] = iw_t[:IDX_HEADS]


def _in_proj(x2, g_attn, w_in, c_tab, s_tab, g_qa, g_ka, g_idx_k, g_qb, g_kb):
    n, d = x2.shape
    ha = A_HEADS * HEAD_DIM
    o = np.cumsum([0, ha, HEAD_DIM, HEAD_DIM, IDX_HEADS * HEAD_DIM, HEAD_DIM, IDX_HEADS,
                   B_HEADS * 2 * HEAD_DIM, B_HEADS * 2 * HEAD_DIM, B_HEADS * B_V_DIM])
    aq, ak, av, iq, ik, iw, bq, bk, bv = [w_in[:, o[i]:o[i + 1]] for i in range(9)]
    pad = jnp.zeros((d, LANES - IDX_HEADS), w_in.dtype)
    w = jnp.concatenate([aq, iq, bq, bk, bv, ak, ak, ik, ik, av, av, iw, pad], axis=1).astype(BF16)
    two = lambda g: jnp.tile(g.reshape(1, HEAD_DIM), (1, 2))
    gains = jnp.concatenate([two(g_qa), two(g_qb), two(g_kb), two(g_ka), two(g_idx_k),
                             jnp.zeros((3, LANES), F32)], axis=0)
    tm = min(PROJ_ROWS, n)
    tk = min(DSA_K, tm)
    wide = 4 * LANES
    row = lambda width: pl.BlockSpec((tm, width), lambda i: (i, 0))
    full = lambda a: pl.BlockSpec(a.shape, lambda i: (0, 0))
    outs = pl.pallas_call(
        _in_proj_kernel,
        out_shape=(jax.ShapeDtypeStruct((n, wide), BF16),) * 5
        + (jax.ShapeDtypeStruct((n, LANES), BF16),) * 2
        + (jax.ShapeDtypeStruct((n // tk, HEAD_DIM, tk), BF16),
           jax.ShapeDtypeStruct((IDX_HEADS, n), F32)),
        grid=(n // tm,),
        in_specs=[row(d), full(g_attn), full(w), row(LANES), row(LANES), full(gains)],
        out_specs=(row(wide),) * 5 + (row(LANES),) * 2
        + (pl.BlockSpec((tm // tk, HEAD_DIM, tk), lambda i: (i, 0, 0)),
           pl.BlockSpec((IDX_HEADS, tm), lambda i: (0, i))),
        compiler_params=_params("arbitrary"),
        name="in_proj",
    )(x2, g_attn, w, c_tab, s_tab, gains)
    return outs


def _diff_attn_kernel(lam_ref, q_ref, k_ref, v_ref, g_ref, o_ref, vt_ref, *, lam_init):
    i = pl.program_id(1)
    tq = q_ref.shape[1]
    tk = vt_ref.shape[3]
    lam = lam_ref[0]
    nkb = (i * tq + tq + tk - 1) // tk
    lane = lax.broadcasted_iota(jnp.int32, (1, LANES), 1)
    lo = lane < HEAD_DIM
    k_chunk0 = lax.broadcasted_iota(jnp.int32, (tk, tq), 0) // CHUNK
    q_chunk = (i * tq + lax.broadcasted_iota(jnp.int32, (tk, tq), 1)) // CHUNK

    @pl.when(i == 0)
    def _():
        for hd in range(B_HEADS):
            for c in range(vt_ref.shape[1]):
                blk = v_ref[0, c * tk:(c + 1) * tk, hd * LANES:(hd + 1) * LANES]
                vt_ref[hd, c] = blk.astype(F32).T.astype(BF16)

    qs = []
    for hd in range(B_HEADS):
        qh = q_ref[0, :, hd * LANES:(hd + 1) * LANES] * jnp.asarray(HEAD_DIM ** -0.5, BF16)
        zero = jnp.zeros_like(qh)
        qs.append(jnp.concatenate([jnp.where(lo, qh, zero), jnp.where(lo, zero, qh)], axis=0))

    def body(j, carry):
        start = pl.multiple_of(j * tk, tk)
        ok = k_chunk0 + j * (tk // CHUNK) <= q_chunk
        out = []
        for hd in range(B_HEADS):
            m, l, acc = carry[hd]
            kj = k_ref[0, pl.ds(start, tk), hd * LANES:(hd + 1) * LANES]
            st = lax.dot_general(kj, qs[hd], NT_DIMS, preferred_element_type=F32)
            s = jnp.concatenate([jnp.where(ok, st[:, :tq], MASKED), jnp.where(ok, st[:, tq:], MASKED)],
                                axis=1)
            m_new = jnp.maximum(m, jnp.max(s, axis=0, keepdims=True))
            alpha = jnp.exp(m - m_new)
            p = jnp.exp(s - m_new)
            l = alpha * l + jnp.sum(p, axis=0, keepdims=True)
            acc = alpha * acc + jnp.dot(vt_ref[hd, j], p.astype(BF16), preferred_element_type=F32)
            out.append((m_new, l, acc))
        return tuple(out)

    init = tuple((jnp.full((1, 2 * tq), MASKED, F32), jnp.zeros((1, 2 * tq), F32),
                  jnp.zeros((B_V_DIM, 2 * tq), F32)) for _ in range(B_HEADS))
    final = lax.fori_loop(0, nkb, body, init)
    for hd in range(B_HEADS):
        cols = slice(hd * LANES, (hd + 1) * LANES)
        _, l, acc = final[hd]
        o = acc / l
        o = o[:, :tq] - lam * o[:, tq:]
        ms = jnp.mean(o * o, axis=0, keepdims=True)
        o = o * lax.rsqrt(ms + NORM_EPS) * g_ref[...] * (1.0 - lam_init)
        o_ref[0, :, cols] = o.T.astype(o_ref.dtype)


def _diff_attn(bq, bk, bv, lam, g_subln, lam_init):
    b, s, w = bq.shape
    tq = min(DIFF_Q, s)
    tk = min(DIFF_K, s)
    kv = pl.BlockSpec((1, s, w), lambda bi, i: (bi, 0, 0))
    qo = pl.BlockSpec((1, tq, w), lambda bi, i: (bi, i, 0))
    return pl.pallas_call(
        functools.partial(_diff_attn_kernel, lam_init=lam_init),
        out_shape=jax.ShapeDtypeStruct((b, s, w), BF16),
        grid=(b, s // tq),
        in_specs=[pl.BlockSpec(memory_space=pltpu.SMEM), qo, kv, kv,
                  pl.BlockSpec((B_V_DIM, 1), lambda bi, i: (0, 0))],
        out_specs=qo,
        scratch_shapes=[pltpu.VMEM((B_HEADS, s // tk, B_V_DIM, tk), BF16)],
        compiler_params=_params("arbitrary", "arbitrary"),
        name="diff_attn",
    )(lam, bq, bk, bv, g_subln.reshape(B_V_DIM, 1))


def _stack_heads(src_ref, dst_ref, scale):
    tq = src_ref.shape[1]
    lane = lax.broadcasted_iota(jnp.int32, (1, LANES), 1)
    lo = lane < HEAD_DIM
    for hd in range(A_HEADS):
        blk = src_ref[0, :, (hd // 2) * LANES:(hd // 2 + 1) * LANES]
        if scale is not None:
            blk = blk * jnp.asarray(scale, blk.dtype)
        keep = lo if hd % 2 == 0 else jnp.logical_not(lo)
        dst_ref[hd * tq:(hd + 1) * tq, :] = jnp.where(keep, blk, jnp.zeros_like(blk))


def _dsa_attn_kernel(aq_ref, iq_ref, kka_ref, kki_ref, vt_ref, iw_ref, o_ref,
                     qa_ref, qi_ref, key_ref, *, k_sel):
    i = pl.program_id(1)
    tq = aq_ref.shape[1]
    tk = key_ref.shape[1]
    nsel = (i * tq + tq + tk - 1) // tk
    _stack_heads(aq_ref, qa_ref, HEAD_DIM ** -0.5)
    _stack_heads(iq_ref, qi_ref, None)
    sub = lax.broadcasted_iota(jnp.int32, (tk, tq), 0)
    k_chunk0 = sub // CHUNK
    q_chunk = (i * tq + lax.broadcasted_iota(jnp.int32, (tk, tq), 1)) // CHUNK

    def score_block(j, _):
        start = pl.multiple_of(j * tk, tk)
        kj = kki_ref[0, pl.ds(start, tk), :]
        st = lax.dot_general(kj, qi_ref[...], NT_DIMS, preferred_element_type=F32)
        score = jnp.zeros((tk, tq), F32)
        for hd in range(IDX_HEADS):
            rel = jnp.maximum(st[:, hd * tq:(hd + 1) * tq], 0.0)
            score = score + rel * iw_ref[hd:hd + 1, :]
        bits = pltpu.bitcast(score, jnp.int32)
        key = jnp.where(bits < 0, jnp.int32(INT_MIN) - bits, bits)
        key_ref[j] = jnp.where(k_chunk0 + j * (tk // CHUNK) > q_chunk, jnp.int32(INT_MIN), key)
        return 0

    lax.fori_loop(0, nsel, score_block, 0)

    def count(pred):
        def body(j, acc):
            hit = jnp.where(pred(key_ref[j], j), 1, 0).astype(jnp.int32)
            return acc + jnp.sum(hit.reshape(tk // 8, 8, tq), axis=0)
        acc = lax.fori_loop(0, nsel, body, jnp.zeros((8, tq), jnp.int32))
        return jnp.sum(acc, axis=0, keepdims=True)

    def search(groups, start):
        if groups * tk <= k_sel:
            return start

        def search_bit(t, thr):
            cand = thr + lax.shift_left(jnp.int32(1), 31 - t)
            acc = jnp.zeros((8, tq), jnp.int32)
            for j in range(groups):
                hit = jnp.where(key_ref[j] >= cand, 1, 0).astype(jnp.int32)
                acc = acc + jnp.sum(hit.reshape(tk // 8, 8, tq), axis=0)
            cnt = jnp.sum(acc, axis=0, keepdims=True)
            return jnp.where(cnt >= k_sel, cand, thr)
        return lax.fori_loop(0, 32, search_bit, start)

    thr = lax.switch(nsel - 1, [functools.partial(search, g + 1) for g in range(key_ref.shape[0])],
                     jnp.full((1, tq), INT_MIN, jnp.int32))
    n_gt = count(lambda key, j: key > thr)
    n_eq = count(lambda key, j: key == thr)
    need = k_sel - n_gt
    tie_break = jnp.logical_and(thr > INT_MIN, n_eq > need)

    def tie_search(_):
        def bit(t, cut):
            cand = cut + lax.shift_left(jnp.int32(1), 15 - t)
            below = count(lambda key, j: jnp.logical_and(key == thr, sub + j * tk < cand))
            return jnp.where(below < need, cand, cut)
        return lax.fori_loop(0, 16, bit, jnp.zeros((1, tq), jnp.int32))

    any_tie = jnp.max(jnp.where(tie_break, 1, 0)) > 0
    cut = lax.cond(any_tie, tie_search, lambda _: jnp.full((1, tq), 2 ** 30, jnp.int32), 0)
    cut = jnp.where(tie_break, cut, 2 ** 30)

    parts = 1
    part_heads = A_HEADS // parts
    part_rows = part_heads * tq

    def attn_block(j, carry):
        start = pl.multiple_of(j * tk, tk)
        kj = kka_ref[0, pl.ds(start, tk), :]
        key = key_ref[j]
        sel = jnp.logical_or(key > thr, jnp.logical_and(key == thr, sub + j * tk <= cut))
        sel = jnp.logical_and(sel, key > INT_MIN)
        out = []
        for part in range(parts):
            m, l, acc = carry[part]
            q_part = qa_ref[part * part_rows:(part + 1) * part_rows, :]
            st = lax.dot_general(kj, q_part, NT_DIMS, preferred_element_type=F32)
            s = jnp.concatenate([jnp.where(sel, st[:, hd * tq:(hd + 1) * tq], MASKED)
                                 for hd in range(part_heads)], axis=1)
            m_new = jnp.maximum(m, jnp.max(s, axis=0, keepdims=True))
            alpha = jnp.exp(m - m_new)
            p = jnp.exp(s - m_new)
            l = alpha * l + jnp.sum(p, axis=0, keepdims=True)
            acc = alpha * acc + jnp.dot(vt_ref[j], p.astype(BF16), preferred_element_type=F32)
            out.append((m_new, l, acc))
        return tuple(out)

    init = tuple((jnp.full((1, part_rows), MASKED, F32), jnp.zeros((1, part_rows), F32),
                  jnp.zeros((HEAD_DIM, part_rows), F32)) for _ in range(parts))
    final = lax.fori_loop(0, nsel, attn_block, init)
    out_t = jnp.concatenate([acc / l for _, l, acc in final], axis=1)
    for g in range(A_HEADS // 2):
        pair = out_t[:, 2 * g * tq:(2 * g + 2) * tq]
        pair = jnp.concatenate([pair[:, :tq], pair[:, tq:]], axis=0)
        o_ref[0, :, g * LANES:(g + 1) * LANES] = pair.T.astype(o_ref.dtype)


def _dsa_attn(aq, iq, kka, kki, v_t, iw_t, k_sel):
    b, s, w = aq.shape
    tq = min(DSA_Q, s)
    tk = v_t.shape[2]
    nq = s // tq
    qo = pl.BlockSpec((1, tq, w), lambda bi, i: (bi, i, 0))
    kv = pl.BlockSpec((1, s, LANES), lambda bi, i: (bi, 0, 0))
    return pl.pallas_call(
        functools.partial(_dsa_attn_kernel, k_sel=k_sel),
        out_shape=jax.ShapeDtypeStruct((b, s, w), BF16),
        grid=(b, nq),
        in_specs=[qo, qo, kv, kv,
                  pl.BlockSpec((s // tk, HEAD_DIM, tk), lambda bi, i: (bi, 0, 0)),
                  pl.BlockSpec((IDX_HEADS, tq), lambda bi, i: (0, bi * nq + i))],
        out_specs=qo,
        scratch_shapes=[pltpu.VMEM((A_HEADS * tq, LANES), BF16),
                        pltpu.VMEM((IDX_HEADS * tq, LANES), BF16),
                        pltpu.VMEM((s // tk, tk, tq), jnp.int32)],
        compiler_params=_params("arbitrary", "arbitrary"),
        name="dsa_attn",
    )(aq, iq, kka, kki, v_t, iw_t)


def _pack_halves(y):
    w = y.shape[1] // 2
    r = y.astype(BF16).astype(F32)
    lo = pltpu.bitcast(r[:, :w], jnp.uint32)
    hi = pltpu.bitcast(r[:, w:], jnp.uint32)
    return jnp.bitwise_or(jnp.bitwise_and(hi, jnp.uint32(0xFFFF0000)), lax.shift_right_logical(lo, jnp.uint32(16)))


def _unpack_halves(u):
    lo = pltpu.bitcast(lax.shift_left(u, jnp.uint32(16)), F32)
    hi = pltpu.bitcast(jnp.bitwise_and(u, jnp.uint32(0xFFFF0000)), F32)
    return lo, hi


SLAB = 4


def _store_slab(ref, first, packed):
    rows = packed.shape[0]
    for c in range(SLAB):
        ref[pl.ds(SLAB * first + c, rows, stride=SLAB), :] = packed[:, c * LANES:(c + 1) * LANES]


def _load_slab(ref, first, rows):
    return jnp.concatenate([ref[pl.ds(SLAB * first + c, rows, stride=SLAB), :] for c in range(SLAB)], axis=1)


def _out_proj_kernel(x_ref, ya_ref, yb_ref, wa_ref, wb_ref, g_ref, wr_ref, br_ref,
                     x1_ref, hp_ref, idx_ref, gate_ref, cnt_ref):
    x1 = (x_ref[...] + jnp.dot(ya_ref[...], wa_ref[...], preferred_element_type=F32)
          + jnp.dot(yb_ref[...], wb_ref[...], preferred_element_type=F32))
    x1_ref[...] = x1
    ms = jnp.mean(x1 * x1, axis=-1, keepdims=True)
    h = x1 * lax.rsqrt(ms + NORM_EPS) * g_ref[...]
    _store_slab(hp_ref, 0, _pack_halves(h))
    logits = jnp.dot(h.astype(BF16), wr_ref[...], preferred_element_type=F32) + br_ref[...]
    tm = logits.shape[0]
    lane = lax.broadcasted_iota(jnp.int32, logits.shape, 1)
    work = jnp.where(lane < N_EXPERTS, logits, -jnp.inf)
    val_out = jnp.full(logits.shape, -jnp.inf, F32)
    member = jnp.zeros(logits.shape, F32)
    args = []
    for r in range(TOP_EXPERTS):
        top = jnp.max(work, axis=-1, keepdims=True)
        arg = jnp.min(jnp.where(work == top, lane, LANES), axis=-1, keepdims=True)
        args.append(arg)
        val_out = jnp.where(lane == r, top, val_out)
        member = jnp.where(lane == arg, 1.0, member)
        work = jnp.where(lane == arg, -jnp.inf, work)
    e = jnp.exp(val_out - jnp.max(val_out, axis=-1, keepdims=True))
    gate_ref[...] = e / jnp.sum(e, axis=-1, keepdims=True)
    earlier = jnp.where(lax.broadcasted_iota(jnp.int32, (tm, tm), 1)
                        < lax.broadcasted_iota(jnp.int32, (tm, tm), 0), 1.0, 0.0).astype(BF16)
    before = jnp.dot(earlier, member.astype(BF16), preferred_element_type=F32)
    idx_out = jnp.zeros(logits.shape, jnp.int32)
    for r in range(TOP_EXPERTS):
        rank = jnp.sum(jnp.where(lane == args[r], before, 0.0), axis=-1, keepdims=True)
        idx_out = jnp.where(lane == r, args[r], idx_out)
        idx_out = jnp.where(lane == TOP_EXPERTS + r, rank.astype(jnp.int32), idx_out)
    idx_ref[...] = idx_out
    counts = jnp.sum(member, axis=0, keepdims=True).astype(jnp.int32)
    cnt_ref[0] = jnp.broadcast_to(counts, (8, LANES))


def _out_proj(x2, y_a, y_b, w_o, g_ffn, w_router, b_router):
    n, d = x2.shape
    da = y_a.shape[1]
    wa = w_o[:da].astype(BF16)
    wb = w_o[da:].astype(BF16)
    wr = jnp.concatenate([w_router, jnp.zeros((d, LANES - N_EXPERTS), w_router.dtype)], 1).astype(BF16)
    br = jnp.concatenate([b_router, jnp.zeros((LANES - N_EXPERTS,), F32)]).reshape(1, LANES)
    tm = min(PROJ_ROWS, n)
    row = lambda width: pl.BlockSpec((tm, width), lambda i: (i, 0))
    full = lambda a: pl.BlockSpec(a.shape, lambda i: (0, 0))
    return pl.pallas_call(
        _out_proj_kernel,
        out_shape=(jax.ShapeDtypeStruct((n, d), F32), jax.ShapeDtypeStruct((SLAB * n, LANES), jnp.uint32),
                   jax.ShapeDtypeStruct((n, LANES), jnp.int32), jax.ShapeDtypeStruct((n, LANES), F32),
                   jax.ShapeDtypeStruct((n // tm, 8, LANES), jnp.int32)),
        grid=(n // tm,),
        in_specs=[row(d), row(da), row(y_b.shape[1]), full(wa), full(wb), full(g_ffn), full(wr), full(br)],
        out_specs=(row(d), pl.BlockSpec((SLAB * tm, LANES), lambda i: (i, 0)), row(LANES), row(LANES),
                   pl.BlockSpec((1, 8, LANES), lambda i: (i, 0, 0))),
        compiler_params=_params("arbitrary"),
        name="out_proj",
    )(x2, y_a, y_b, wa, wb, g_ffn, wr, br)


def _moe_plan(top_idx, counts, tile_rows):
    n = top_idx.shape[0]
    rows = MOE_ROWS
    n_blocks = -(-(n * TOP_EXPERTS + N_EXPERTS * (rows - 1)) // rows)
    total = jnp.sum(counts, axis=0)
    padded = ((total + rows - 1) // rows) * rows
    pend = jnp.cumsum(padded)
    base = (pend - padded)[None, :] + jnp.cumsum(counts, axis=0) - counts
    expert = top_idx[:, :TOP_EXPERTS]
    rank = top_idx[:, TOP_EXPERTS:2 * TOP_EXPERTS]
    base_tok = jnp.repeat(base, tile_rows, axis=0)
    hit = expert[:, :, None] == jnp.arange(N_EXPERTS, dtype=jnp.int32)[None, None, :]
    dest = jnp.sum(jnp.where(hit, base_tok[:, None, :], 0), axis=-1) + rank
    block_start = jnp.arange(n_blocks, dtype=jnp.int32) * rows
    block_expert = jnp.minimum(jnp.sum((block_start[:, None] >= pend[None, :]).astype(jnp.int32), axis=1),
                               N_EXPERTS - 1).astype(jnp.int32)
    n_used = (pend[-1] // rows).astype(jnp.int32).reshape(1)
    return dest.astype(jnp.int32).reshape(-1), block_expert, n_used, n_blocks


def _dispatch_kernel(dest_ref, h_ref, xs_in, xs_hbm, sem):
    del xs_in
    per_step = dest_ref.shape[2]
    tm = per_step // TOP_EXPERTS

    def issue(g, _):
        first = g * (ISSUE_GROUP * TOP_EXPERTS)
        slots = [dest_ref[0, 0, first + j] for j in range(ISSUE_GROUP * TOP_EXPERTS)]
        for j, slot in enumerate(slots):
            t = g * ISSUE_GROUP + j // TOP_EXPERTS
            src = h_ref.at[pl.ds(pl.multiple_of(SLAB * t, SLAB), SLAB)]
            dst = xs_hbm.at[pl.ds(pl.multiple_of(SLAB * slot, SLAB), SLAB)]
            pltpu.make_async_copy(src, dst, sem).start(priority=j % 2)
        return 0

    lax.fori_loop(0, tm // ISSUE_GROUP, issue, 0)
    whole = xs_hbm.at[pl.ds(0, SLAB * per_step)]
    pltpu.make_async_copy(whole, whole, sem).wait()


def _moe_dispatch(hp, dest, m):
    n = hp.shape[0] // SLAB
    tm = min(PROJ_ROWS, n)
    per_step = tm * TOP_EXPERTS
    return pl.pallas_call(
        _dispatch_kernel,
        out_shape=jax.ShapeDtypeStruct((SLAB * m, LANES), jnp.uint32),
        grid=(n // tm,),
        in_specs=[pl.BlockSpec((1, 1, per_step), lambda i: (i, 0, 0), memory_space=pltpu.SMEM),
                  pl.BlockSpec((SLAB * tm, LANES), lambda i: (i, 0)), pl.BlockSpec(memory_space=pl.ANY)],
        out_specs=pl.BlockSpec(memory_space=pl.ANY),
        scratch_shapes=[pltpu.SemaphoreType.DMA],
        input_output_aliases={2: 0},
        compiler_params=_params("arbitrary"),
        name="moe_dispatch",
    )(dest.reshape(n // tm, 1, per_step), hp, jnp.zeros((SLAB * m, LANES), jnp.uint32))


def _moe_kernel(bexp_ref, nused_ref, x_ref, wgu_ref, bgu_ref, wd_ref, bd_ref, y_ref, wgu_bf, wd_bf):
    i = pl.program_id(0)
    rows = x_ref.shape[0] // SLAB
    half = SLAB * LANES
    f = wd_ref.shape[1]

    @pl.when(jnp.logical_or(i == 0, bexp_ref[i] != bexp_ref[jnp.maximum(i - 1, 0)]))
    def _():
        wgu_bf[...] = wgu_ref[0].astype(BF16)
        wd_bf[...] = wd_ref[0].astype(BF16)

    @pl.when(i < nused_ref[0])
    def _():
        lo, hi = _unpack_halves(_load_slab(x_ref, 0, rows))
        gu = (jnp.dot(lo.astype(BF16), wgu_bf[:half, :], preferred_element_type=F32)
              + jnp.dot(hi.astype(BF16), wgu_bf[half:, :], preferred_element_type=F32)
              + bgu_ref[0])
        glu = jnp.minimum(gu[:, :f], SWIGLU_LIMIT)
        lin = jnp.clip(gu[:, f:], -SWIGLU_LIMIT, SWIGLU_LIMIT)
        act = glu * jax.nn.sigmoid(SWIGLU_ALPHA * glu) * (lin + 1.0)
        y = jnp.dot(act.astype(BF16), wd_bf[...], preferred_element_type=F32) + bd_ref[0]
        _store_slab(y_ref, 0, _pack_halves(y))

    @pl.when(i >= nused_ref[0])
    def _():
        y_ref[...] = jnp.zeros(y_ref.shape, y_ref.dtype)


def _moe_experts(xs, block_expert, n_used, w_gu, b_gu, w_down, b_down):
    m = xs.shape[0] // SLAB
    d = 2 * SLAB * LANES
    f = w_down.shape[1]
    rows = MOE_ROWS
    bgu = b_gu.reshape(N_EXPERTS, 1, 2 * f)
    bd = b_down.reshape(N_EXPERTS, 1, d)
    slots = pl.BlockSpec((SLAB * rows, LANES), lambda i, be, nu: (i, 0))
    by_expert = lambda shape: pl.BlockSpec((1,) + shape, lambda i, be, nu: (be[i], 0, 0))
    return pl.pallas_call(
        _moe_kernel,
        out_shape=jax.ShapeDtypeStruct((SLAB * m, LANES), jnp.uint32),
        grid_spec=pltpu.PrefetchScalarGridSpec(
            num_scalar_prefetch=2,
            grid=(m // rows,),
            in_specs=[slots, by_expert((d, 2 * f)), by_expert((1, 2 * f)),
                      by_expert((f, d)), by_expert((1, d))],
            out_specs=slots,
            scratch_shapes=[pltpu.VMEM((d, 2 * f), BF16), pltpu.VMEM((f, d), BF16)]),
        compiler_params=pltpu.CompilerParams(dimension_semantics=("arbitrary",),
                                             vmem_limit_bytes=MOE_VMEM_LIMIT),
        name="moe_experts",
    )(block_expert, n_used, xs, w_gu, bgu, w_down, bd)


def _combine_kernel(dcur_ref, dnext_ref, y_hbm, x_ref, gate_ref, o_ref, ybuf, sems):
    i = pl.program_id(0)
    nsteps = pl.num_programs(0)
    tm = x_ref.shape[0]
    half = SLAB * LANES
    slot = lax.rem(i, 2)

    def issue(dest_ref, buf):
        def body(g, _):
            first = g * (ISSUE_GROUP * TOP_EXPERTS)
            slots = [dest_ref[0, 0, first + j] for j in range(ISSUE_GROUP * TOP_EXPERTS)]
            for j, slot in enumerate(slots):
                t = g * ISSUE_GROUP + j // TOP_EXPERTS
                src = pl.multiple_of(SLAB * slot, SLAB)
                dst = pl.multiple_of(SLAB * ((j % TOP_EXPERTS) * tm + t), SLAB)
                pltpu.make_async_copy(y_hbm.at[pl.ds(src, SLAB)], ybuf.at[buf, pl.ds(dst, SLAB)],
                                      sems.at[buf]).start(priority=j % 2)
            return 0
        lax.fori_loop(0, tm // ISSUE_GROUP, body, 0)

    @pl.when(i == 0)
    def _():
        issue(dcur_ref, 0)

    @pl.when(i + 1 < nsteps)
    def _():
        issue(dnext_ref, 1 - slot)

    pltpu.make_async_copy(y_hbm.at[pl.ds(0, SLAB * TOP_EXPERTS * tm)], ybuf.at[slot], sems.at[slot]).wait()
    lo = x_ref[:, :half]
    hi = x_ref[:, half:]
    for k in range(TOP_EXPERTS):
        a, b = _unpack_halves(_load_slab(ybuf.at[slot], k * tm, tm))
        g = gate_ref[:, k:k + 1]
        lo = lo + a * g
        hi = hi + b * g
    o_ref[:, :half] = lo
    o_ref[:, half:] = hi


def _moe_combine(x1, y_rows, dest, gates):
    n, d = x1.shape
    tm = min(COMBINE_ROWS, n)
    nt = n // tm
    per_step = tm * TOP_EXPERTS
    dest3 = dest.reshape(nt, 1, per_step)
    return pl.pallas_call(
        _combine_kernel,
        out_shape=jax.ShapeDtypeStruct((n, d), F32),
        grid=(nt,),
        in_specs=[pl.BlockSpec((1, 1, per_step), lambda i: (i, 0, 0), memory_space=pltpu.SMEM),
                  pl.BlockSpec((1, 1, per_step), lambda i: (jnp.minimum(i + 1, nt - 1), 0, 0),
                               memory_space=pltpu.SMEM),
                  pl.BlockSpec(memory_space=pl.ANY),
                  pl.BlockSpec((tm, d), lambda i: (i, 0)),
                  pl.BlockSpec((tm, LANES), lambda i: (i, 0))],
        out_specs=pl.BlockSpec((tm, d), lambda i: (i, 0)),
        scratch_shapes=[pltpu.VMEM((2, SLAB * per_step, LANES), jnp.uint32), pltpu.SemaphoreType.DMA((2,))],
        compiler_params=_params("arbitrary"),
        name="moe_combine",
    )(dest3, dest3, y_rows, x1, gates)


def kernel(x, positions, g_attn, w_in, g_qa, g_ka, g_idx_k, lambda_q1, lambda_k1, lambda_q2, lambda_k2, g_qb, g_kb, g_subln, w_o, g_ffn, w_router, b_router, w_gu, b_gu, w_down, b_down):
    b, s, d = x.shape
    n = b * s
    k_sel = min(TOPK_KEYS, s // 4)
    c_tab, s_tab = _rope_tables(positions)
    for layer in range(g_attn.shape[0]):
        lam_init = 0.8 - 0.6 * math.exp(-0.3 * layer)
        x2 = x.reshape(n, d)
        aq, iq, bq, bk, bv, kka, kki, v_t, iw_t = _in_proj(
            x2, g_attn[layer].reshape(1, d), w_in[layer], c_tab, s_tab,
            g_qa[layer], g_ka[layer], g_idx_k[layer], g_qb[layer], g_kb[layer])
        seq = lambda t: t.reshape(b, s, t.shape[-1])
        y_a = _dsa_attn(seq(aq), seq(iq), seq(kka), seq(kki), v_t, iw_t, k_sel)
        lam = (jnp.exp(jnp.sum(lambda_q1[layer].astype(F32) * lambda_k1[layer].astype(F32)))
               - jnp.exp(jnp.sum(lambda_q2[layer].astype(F32) * lambda_k2[layer].astype(F32)))
               + lam_init).reshape(1)
        y_b = _diff_attn(seq(bq), seq(bk), seq(bv), lam, g_subln[layer].reshape(1, B_V_DIM), lam_init)
        x1, hp, top_idx, gates, counts = _out_proj(
            x2, y_a.reshape(n, -1), y_b.reshape(n, -1), w_o[layer], g_ffn[layer].reshape(1, d),
            w_router[layer], b_router[layer])
        dest, block_expert, n_used, n_blocks = _moe_plan(
            top_idx, counts[:, 0, :N_EXPERTS], n // counts.shape[0])
        xs = _moe_dispatch(hp, dest, n_blocks * MOE_ROWS)
        y_rows = _moe_experts(xs, block_expert, n_used, w_gu[layer], b_gu[layer], w_down[layer], b_down[layer])
        x = _moe_combine(x1, y_rows, dest, gates).reshape(b, s, d)
    return x
```

```python
import functools
import math

import jax
import jax.numpy as jnp
import numpy as np
from jax import lax
from jax.experimental import pallas as pl
from jax.experimental.pallas import tpu as pltpu

CHUNK = 64
ROPE_THETA = 500000.0
NORM_EPS = 1e-6
A_HEADS = 8
HEAD_DIM = 64
IDX_HEADS = 8
TOPK_KEYS = 256
B_HEADS = 4
B_V_DIM = 128
ROT_DIM = HEAD_DIM // 4
ROT_HALF = ROT_DIM // 2
N_EXPERTS = 32
TOP_EXPERTS = 4
SWIGLU_LIMIT = 7.0
SWIGLU_ALPHA = 1.702

LANES = 128
VMEM_LIMIT = 48 * 1024 * 1024
MOE_VMEM_LIMIT = 56 * 1024 * 1024

PROJ_ROWS = 512
DIFF_Q = 256
DIFF_K = 512
DSA_Q = 256
DSA_K = 256
MOE_ROWS = 512
COMBINE_ROWS = 256
ISSUE_GROUP = 4

MASKED = -1e30
LOGIT_SCALE = HEAD_DIM ** -0.5 * math.log2(math.e)
INT_MIN = -(2 ** 31)

BF16 = jnp.bfloat16
F32 = jnp.float32
NT_DIMS = (((1,), (1,)), ((), ()))


def _params(*sem):
    return pltpu.CompilerParams(dimension_semantics=sem, vmem_limit_bytes=VMEM_LIMIT)


def _rope_kernel(invf_ref, pos_ref, cos_ref, sin_ref):
    f = pl.program_id(0)
    ang = pos_ref[...].astype(F32) * invf_ref[f]
    cos_ref[0] = jnp.cos(ang)
    sin_ref[0] = jnp.sin(ang)


def _rope_tables(positions):
    b, s = positions.shape
    inv_freq = ROPE_THETA ** (-jnp.arange(0, ROT_DIM, 2, dtype=F32) / ROT_DIM)
    cos_t, sin_t = pl.pallas_call(
        _rope_kernel,
        out_shape=(jax.ShapeDtypeStruct((ROT_HALF, b, s), F32),) * 2,
        grid=(ROT_HALF,),
        in_specs=[pl.BlockSpec(memory_space=pltpu.SMEM),
                  pl.BlockSpec((b, s), lambda f: (0, 0))],
        out_specs=(pl.BlockSpec((1, b, s), lambda f: (f, 0, 0)),) * 2,
        compiler_params=_params("arbitrary"),
        name="rope_tables",
    )(inv_freq, positions)
    n = b * s
    cos8 = jnp.transpose(cos_t, (1, 2, 0)).reshape(n, ROT_HALF)
    sin8 = jnp.transpose(sin_t, (1, 2, 0)).reshape(n, ROT_HALF)
    ones = jnp.ones((n, HEAD_DIM - ROT_DIM), F32)
    zeros = jnp.zeros((n, HEAD_DIM - ROT_DIM), F32)
    c_tab = jnp.concatenate([cos8, cos8, ones], axis=1)
    s_tab = jnp.concatenate([-sin8, sin8, zeros], axis=1)
    return jnp.tile(c_tab, (1, 2)), jnp.tile(s_tab, (1, 2))


def _segment_rms(y, gain):
    lane = lax.broadcasted_iota(jnp.int32, (1, LANES), 1)
    lo = lane < HEAD_DIM
    y2 = y * y
    s0 = jnp.sum(jnp.where(lo, y2, 0.0), axis=-1, keepdims=True)
    s1 = jnp.sum(jnp.where(lo, 0.0, y2), axis=-1, keepdims=True)
    ms = jnp.where(lo, s0, s1) * (1.0 / HEAD_DIM)
    return y * lax.rsqrt(ms + NORM_EPS) * gain


def _rope128(y, c_tab, s_tab):
    lane = lax.broadcasted_iota(jnp.int32, (1, LANES), 1)
    first = (lane % HEAD_DIM) < ROT_HALF
    up = pltpu.roll(y, LANES - ROT_HALF, 1)
    down = pltpu.roll(y, ROT_HALF, 1)
    return y * c_tab + jnp.where(first, up, down) * s_tab


def _in_proj_kernel(x_ref, g_ref, w_ref, c_ref, s_ref, gains_ref,
                    aq_ref, iq_ref, bq_ref, bk_ref, bv_ref, kka_ref, kki_ref, vt_ref, iw_ref):
    x = x_ref[...]
    ms = jnp.mean(x * x, axis=-1, keepdims=True)
    h = (x * lax.rsqrt(ms + NORM_EPS) * g_ref[...]).astype(BF16)
    c_tab = c_ref[...]
    s_tab = s_ref[...]
    wide = 4 * LANES

    def group(col, out_ref, gain_row, rope):
        acc = jnp.dot(h, w_ref[:, col:col + wide], preferred_element_type=F32)
        for t in range(4):
            y = acc[:, t * LANES:(t + 1) * LANES]
            if gain_row is not None:
                y = _segment_rms(y, gains_ref[gain_row:gain_row + 1, :])
            if rope:
                y = _rope128(y, c_tab, s_tab)
            out_ref[:, t * LANES:(t + 1) * LANES] = y.astype(out_ref.dtype)

    group(0 * wide, aq_ref, 0, True)
    group(1 * wide, iq_ref, None, True)
    group(2 * wide, bq_ref, 1, True)
    group(3 * wide, bk_ref, 2, True)
    group(4 * wide, bv_ref, None, False)
    acc = jnp.dot(h, w_ref[:, 5 * wide:6 * wide], preferred_element_type=F32)
    ka = _rope128(_segment_rms(acc[:, 0:LANES], gains_ref[3:4, :]), c_tab, s_tab)
    ki = _rope128(_segment_rms(acc[:, LANES:2 * LANES], gains_ref[4:5, :]), c_tab, s_tab)
    kka_ref[...] = ka.astype(BF16)
    kki_ref[...] = ki.astype(BF16)
    tk = vt_ref.shape[2]
    for c in range(vt_ref.shape[0]):
        v_t = acc[c * tk:(c + 1) * tk, 2 * LANES:3 * LANES].T
        vt_ref[c] = v_t[:HEAD_DIM].astype(BF16)
    iw_t = (acc[:, 3 * LANES:4 * LANES] * ((IDX_HEADS * HEAD_DIM) ** -0.5)).T
    iw_ref[...] = iw_t[:IDX_HEADS]


def _in_proj(x2, g_attn, w_in, c_tab, s_tab, g_qa, g_ka, g_idx_k, g_qb, g_kb):
    n, d = x2.shape
    ha = A_HEADS * HEAD_DIM
    o = np.cumsum([0, ha, HEAD_DIM, HEAD_DIM, IDX_HEADS * HEAD_DIM, HEAD_DIM, IDX_HEADS,
                   B_HEADS * 2 * HEAD_DIM, B_HEADS * 2 * HEAD_DIM, B_HEADS * B_V_DIM])
    aq, ak, av, iq, ik, iw, bq, bk, bv = [w_in[:, o[i]:o[i + 1]] for i in range(9)]
    pad = jnp.zeros((d, LANES - IDX_HEADS), w_in.dtype)
    w = jnp.concatenate([aq, iq, bq, bk, bv, ak, ak, ik, ik, av, av, iw, pad], axis=1).astype(BF16)
    two = lambda g: jnp.tile(g.reshape(1, HEAD_DIM), (1, 2))
    gains = jnp.concatenate([two(g_qa) * LOGIT_SCALE, two(g_qb) * LOGIT_SCALE,
                             two(g_kb), two(g_ka), two(g_idx_k),
                             jnp.zeros((3, LANES), F32)], axis=0)
    tm = min(PROJ_ROWS, n)
    tk = min(DSA_K, tm)
    wide = 4 * LANES
    row = lambda width: pl.BlockSpec((tm, width), lambda i: (i, 0))
    full = lambda a: pl.BlockSpec(a.shape, lambda i: (0, 0))
    outs = pl.pallas_call(
        _in_proj_kernel,
        out_shape=(jax.ShapeDtypeStruct((n, wide), BF16),) * 5
        + (jax.ShapeDtypeStruct((n, LANES), BF16),) * 2
        + (jax.ShapeDtypeStruct((n // tk, HEAD_DIM, tk), BF16),
           jax.ShapeDtypeStruct((IDX_HEADS, n), F32)),
        grid=(n // tm,),
        in_specs=[row(d), full(g_attn), full(w), row(LANES), row(LANES), full(gains)],
        out_specs=(row(wide),) * 5 + (row(LANES),) * 2
        + (pl.BlockSpec((tm // tk, HEAD_DIM, tk), lambda i: (i, 0, 0)),
           pl.BlockSpec((IDX_HEADS, tm), lambda i: (0, i))),
        compiler_params=_params("arbitrary"),
        name="in_proj",
    )(x2, g_attn, w, c_tab, s_tab, gains)
    return outs


def _diff_attn_kernel(lam_ref, q_ref, k_ref, v_ref, g_ref, o_ref, vt_ref, *, lam_init):
    i = pl.program_id(1)
    tq = q_ref.shape[1]
    tk = vt_ref.shape[3]
    lam = lam_ref[0]
    nkb = (i * tq + tq + tk - 1) // tk
    lane = lax.broadcasted_iota(jnp.int32, (1, LANES), 1)
    lo = lane < HEAD_DIM
    k_chunk0 = lax.broadcasted_iota(jnp.int32, (tk, tq), 0) // CHUNK
    q_chunk = (i * tq + lax.broadcasted_iota(jnp.int32, (tk, tq), 1)) // CHUNK

    @pl.when(i == 0)
    def _():
        for hd in range(B_HEADS):
            for c in range(vt_ref.shape[1]):
                blk = v_ref[0, c * tk:(c + 1) * tk, hd * LANES:(hd + 1) * LANES]
                vt_ref[hd, c] = blk.astype(F32).T.astype(BF16)

    qs = []
    for hd in range(B_HEADS):
        qh = q_ref[0, :, hd * LANES:(hd + 1) * LANES]
        zero = jnp.zeros_like(qh)
        qs.append(jnp.concatenate([jnp.where(lo, qh, zero), jnp.where(lo, zero, qh)], axis=0))

    def body(j, carry):
        start = pl.multiple_of(j * tk, tk)
        ok = k_chunk0 + j * (tk // CHUNK) <= q_chunk
        out = []
        for hd in range(B_HEADS):
            m, l, acc = carry[hd]
            kj = k_ref[0, pl.ds(start, tk), hd * LANES:(hd + 1) * LANES]
            st = lax.dot_general(kj, qs[hd], NT_DIMS, preferred_element_type=F32)
            m_out, a_out, l_out, p_out = [], [], [], []
            for c in range(2 * tq // LANES):
                cols = slice(c * LANES, (c + 1) * LANES)
                qcols = slice((c * LANES) % tq, (c * LANES) % tq + LANES)
                s = jnp.where(ok[:, qcols], st[:, cols], MASKED)
                m_new = jnp.maximum(m[:, cols], jnp.max(s, axis=0, keepdims=True))
                alpha = jnp.exp2(m[:, cols] - m_new)
                p = jnp.exp2(s - m_new)
                m_out.append(m_new)
                a_out.append(alpha)
                l_out.append(alpha * l[:, cols] + jnp.sum(p, axis=0, keepdims=True))
                p_out.append(p.astype(BF16))
            pv = jnp.dot(vt_ref[hd, j], jnp.concatenate(p_out, axis=1), preferred_element_type=F32)
            out.append((jnp.concatenate(m_out, axis=1), jnp.concatenate(l_out, axis=1),
                        jnp.concatenate(a_out, axis=1) * acc + pv))
        return tuple(out)

    init = tuple((jnp.full((1, 2 * tq), MASKED, F32), jnp.zeros((1, 2 * tq), F32),
                  jnp.zeros((B_V_DIM, 2 * tq), F32)) for _ in range(B_HEADS))
    final = lax.fori_loop(0, nkb, body, init)
    for hd in range(B_HEADS):
        cols = slice(hd * LANES, (hd + 1) * LANES)
        _, l, acc = final[hd]
        o = acc / l
        o = o[:, :tq] - lam * o[:, tq:]
        ms = jnp.mean(o * o, axis=0, keepdims=True)
        o = o * lax.rsqrt(ms + NORM_EPS) * g_ref[...] * (1.0 - lam_init)
        o_ref[0, :, cols] = o.T.astype(o_ref.dtype)


def _diff_attn(bq, bk, bv, lam, g_subln, lam_init):
    b, s, w = bq.shape
    tq = min(DIFF_Q, s)
    tk = min(DIFF_K, s)
    kv = pl.BlockSpec((1, s, w), lambda bi, i: (bi, 0, 0))
    qo = pl.BlockSpec((1, tq, w), lambda bi, i: (bi, i, 0))
    return pl.pallas_call(
        functools.partial(_diff_attn_kernel, lam_init=lam_init),
        out_shape=jax.ShapeDtypeStruct((b, s, w), BF16),
        grid=(b, s // tq),
        in_specs=[pl.BlockSpec(memory_space=pltpu.SMEM), qo, kv, kv,
                  pl.BlockSpec((B_V_DIM, 1), lambda bi, i: (0, 0))],
        out_specs=qo,
        scratch_shapes=[pltpu.VMEM((B_HEADS, s // tk, B_V_DIM, tk), BF16)],
        compiler_params=_params("arbitrary", "arbitrary"),
        name="diff_attn",
    )(lam, bq, bk, bv, g_subln.reshape(B_V_DIM, 1))


def _stack_heads(src_ref, dst_ref):
    tq = src_ref.shape[1]
    lane = lax.broadcasted_iota(jnp.int32, (1, LANES), 1)
    lo = lane < HEAD_DIM
    for hd in range(A_HEADS):
        blk = src_ref[0, :, (hd // 2) * LANES:(hd // 2 + 1) * LANES]
        keep = lo if hd % 2 == 0 else jnp.logical_not(lo)
        dst_ref[hd * tq:(hd + 1) * tq, :] = jnp.where(keep, blk, jnp.zeros_like(blk))


def _dsa_attn_kernel(aq_ref, iq_ref, kka_ref, kki_ref, vt_ref, iw_ref, o_ref,
                     qa_ref, qi_ref, key_ref, *, k_sel):
    i = pl.program_id(1)
    tq = aq_ref.shape[1]
    tk = key_ref.shape[1]
    nsel = (i * tq + tq + tk - 1) // tk
    _stack_heads(aq_ref, qa_ref)
    _stack_heads(iq_ref, qi_ref)
    sub = lax.broadcasted_iota(jnp.int32, (tk, tq), 0)
    k_chunk0 = sub // CHUNK
    q_chunk = (i * tq + lax.broadcasted_iota(jnp.int32, (tk, tq), 1)) // CHUNK

    def score_block(j, _):
        start = pl.multiple_of(j * tk, tk)
        kj = kki_ref[0, pl.ds(start, tk), :]
        st = lax.dot_general(kj, qi_ref[...], NT_DIMS, preferred_element_type=F32)
        score = jnp.zeros((tk, tq), F32)
        for hd in range(IDX_HEADS):
            rel = jnp.maximum(st[:, hd * tq:(hd + 1) * tq], 0.0)
            score = score + rel * iw_ref[hd:hd + 1, :]
        bits = pltpu.bitcast(score, jnp.int32)
        key = jnp.where(bits < 0, jnp.int32(INT_MIN) - bits, bits)
        key_ref[j] = jnp.where(k_chunk0 + j * (tk // CHUNK) > q_chunk, jnp.int32(INT_MIN), key)
        return 0

    lax.fori_loop(0, nsel, score_block, 0)

    def count(pred):
        def body(j, acc):
            hit = jnp.where(pred(key_ref[j], j), 1, 0).astype(jnp.int32)
            return acc + jnp.sum(hit.reshape(tk // 8, 8, tq), axis=0)
        acc = lax.fori_loop(0, nsel, body, jnp.zeros((8, tq), jnp.int32))
        return jnp.sum(acc, axis=0, keepdims=True)

    def search(groups, start):
        if groups * tk <= k_sel:
            return start

        def search_bit(t, thr):
            cand = thr + lax.shift_left(jnp.int32(1), 31 - t)
            acc = jnp.zeros((8, tq), jnp.int32)
            for j in range(groups):
                hit = jnp.where(key_ref[j] >= cand, 1, 0).astype(jnp.int32)
                acc = acc + jnp.sum(hit.reshape(tk // 8, 8, tq), axis=0)
            cnt = jnp.sum(acc, axis=0, keepdims=True)
            return jnp.where(cnt >= k_sel, cand, thr)
        return lax.fori_loop(0, 32, search_bit, start)

    thr = lax.switch(nsel - 1, [functools.partial(search, g + 1) for g in range(key_ref.shape[0])],
                     jnp.full((1, tq), INT_MIN, jnp.int32))
    n_gt = count(lambda key, j: key > thr)
    n_eq = count(lambda key, j: key == thr)
    need = k_sel - n_gt
    tie_break = jnp.logical_and(thr > INT_MIN, n_eq > need)

    def tie_search(_):
        def bit(t, cut):
            cand = cut + lax.shift_left(jnp.int32(1), 15 - t)
            below = count(lambda key, j: jnp.logical_and(key == thr, sub + j * tk < cand))
            return jnp.where(below < need, cand, cut)
        return lax.fori_loop(0, 16, bit, jnp.zeros((1, tq), jnp.int32))

    any_tie = jnp.max(jnp.where(tie_break, 1, 0)) > 0
    cut = lax.cond(any_tie, tie_search, lambda _: jnp.full((1, tq), 2 ** 30, jnp.int32), 0)
    cut = jnp.where(tie_break, cut, 2 ** 30)

    rows = A_HEADS * tq

    def attn_block(j, carry):
        m, l, acc = carry
        start = pl.multiple_of(j * tk, tk)
        kj = kka_ref[0, pl.ds(start, tk), :]
        key = key_ref[j]
        sel = jnp.logical_or(key > thr, jnp.logical_and(key == thr, sub + j * tk <= cut))
        sel = jnp.logical_and(sel, key > INT_MIN)
        st = lax.dot_general(kj, qa_ref[...], NT_DIMS, preferred_element_type=F32)
        m_out, a_out, l_out, p_out = [], [], [], []
        for c in range(rows // LANES):
            cols = slice(c * LANES, (c + 1) * LANES)
            qcols = slice((c * LANES) % tq, (c * LANES) % tq + LANES)
            s = jnp.where(sel[:, qcols], st[:, cols], MASKED)
            m_new = jnp.maximum(m[:, cols], jnp.max(s, axis=0, keepdims=True))
            alpha = jnp.exp2(m[:, cols] - m_new)
            p = jnp.exp2(s - m_new)
            m_out.append(m_new)
            a_out.append(alpha)
            l_out.append(alpha * l[:, cols] + jnp.sum(p, axis=0, keepdims=True))
            p_out.append(p.astype(BF16))
        alpha = jnp.concatenate(a_out, axis=1)
        pv = jnp.dot(vt_ref[j], jnp.concatenate(p_out, axis=1), preferred_element_type=F32)
        return jnp.concatenate(m_out, axis=1), jnp.concatenate(l_out, axis=1), alpha * acc + pv

    init = (jnp.full((1, rows), MASKED, F32), jnp.zeros((1, rows), F32), jnp.zeros((HEAD_DIM, rows), F32))
    _, l, acc = lax.fori_loop(0, nsel, attn_block, init)
    out_t = acc / l
    for g in range(A_HEADS // 2):
        pair = out_t[:, 2 * g * tq:(2 * g + 2) * tq]
        pair = jnp.concatenate([pair[:, :tq], pair[:, tq:]], axis=0)
        o_ref[0, :, g * LANES:(g + 1) * LANES] = pair.T.astype(o_ref.dtype)


def _dsa_attn(aq, iq, kka, kki, v_t, iw_t, k_sel):
    b, s, w = aq.shape
    tq = min(DSA_Q, s)
    tk = v_t.shape[2]
    nq = s // tq
    qo = pl.BlockSpec((1, tq, w), lambda bi, i: (bi, i, 0))
    kv = pl.BlockSpec((1, s, LANES), lambda bi, i: (bi, 0, 0))
    return pl.pallas_call(
        functools.partial(_dsa_attn_kernel, k_sel=k_sel),
        out_shape=jax.ShapeDtypeStruct((b, s, w), BF16),
        grid=(b, nq),
        in_specs=[qo, qo, kv, kv,
                  pl.BlockSpec((s // tk, HEAD_DIM, tk), lambda bi, i: (bi, 0, 0)),
                  pl.BlockSpec((IDX_HEADS, tq), lambda bi, i: (0, bi * nq + i))],
        out_specs=qo,
        scratch_shapes=[pltpu.VMEM((A_HEADS * tq, LANES), BF16),
                        pltpu.VMEM((IDX_HEADS * tq, LANES), BF16),
                        pltpu.VMEM((s // tk, tk, tq), jnp.int32)],
        compiler_params=_params("arbitrary", "arbitrary"),
        name="dsa_attn",
    )(aq, iq, kka, kki, v_t, iw_t)


def _pack_halves(y):
    w = y.shape[1] // 2
    r = y.astype(BF16).astype(F32)
    lo = pltpu.bitcast(r[:, :w], jnp.uint32)
    hi = pltpu.bitcast(r[:, w:], jnp.uint32)
    return jnp.bitwise_or(jnp.bitwise_and(hi, jnp.uint32(0xFFFF0000)), lax.shift_right_logical(lo, jnp.uint32(16)))


def _unpack_halves(u):
    lo = pltpu.bitcast(lax.shift_left(u, jnp.uint32(16)), F32)
    hi = pltpu.bitcast(jnp.bitwise_and(u, jnp.uint32(0xFFFF0000)), F32)
    return lo, hi


SLAB = 4


def _store_slab(ref, first, packed):
    rows = packed.shape[0]
    for c in range(SLAB):
        ref[pl.ds(SLAB * first + c, rows, stride=SLAB), :] = packed[:, c * LANES:(c + 1) * LANES]


def _load_slab(ref, first, rows):
    return jnp.concatenate([ref[pl.ds(SLAB * first + c, rows, stride=SLAB), :] for c in range(SLAB)], axis=1)


def _out_proj_kernel(x_ref, ya_ref, yb_ref, wa_ref, wb_ref, g_ref, wr_ref, br_ref,
                     x1_ref, hp_ref, idx_ref, gate_ref, cnt_ref):
    tm = x_ref.shape[0]
    parts = 2 if tm % 16 == 0 else 1
    pr = tm // parts
    lane = lax.broadcasted_iota(jnp.int32, (pr, LANES), 1)
    members, all_args = [], []
    for part in range(parts):
        r0 = part * pr
        x1 = (x_ref[r0:r0 + pr, :]
              + jnp.dot(ya_ref[r0:r0 + pr, :], wa_ref[...], preferred_element_type=F32)
              + jnp.dot(yb_ref[r0:r0 + pr, :], wb_ref[...], preferred_element_type=F32))
        x1_ref[r0:r0 + pr, :] = x1
        ms = jnp.mean(x1 * x1, axis=-1, keepdims=True)
        h = x1 * lax.rsqrt(ms + NORM_EPS) * g_ref[...]
        _store_slab(hp_ref, r0, _pack_halves(h))
        logits = jnp.dot(h.astype(BF16), wr_ref[...], preferred_element_type=F32) + br_ref[...]
        work = jnp.where(lane < N_EXPERTS, logits, -jnp.inf)
        val_out = jnp.full(logits.shape, -jnp.inf, F32)
        member = jnp.zeros(logits.shape, F32)
        args = []
        for r in range(TOP_EXPERTS):
            top = jnp.max(work, axis=-1, keepdims=True)
            arg = jnp.min(jnp.where(work == top, lane, LANES), axis=-1, keepdims=True)
            args.append(arg)
            val_out = jnp.where(lane == r, top, val_out)
            member = jnp.where(lane == arg, 1.0, member)
            work = jnp.where(lane == arg, -jnp.inf, work)
        e = jnp.exp(val_out - jnp.max(val_out, axis=-1, keepdims=True))
        gate_ref[r0:r0 + pr, :] = e / jnp.sum(e, axis=-1, keepdims=True)
        members.append(member)
        all_args.append(args)
    member = jnp.concatenate(members, axis=0)
    earlier = jnp.where(lax.broadcasted_iota(jnp.int32, (tm, tm), 1)
                        < lax.broadcasted_iota(jnp.int32, (tm, tm), 0), 1.0, 0.0).astype(BF16)
    before = jnp.dot(earlier, member.astype(BF16), preferred_element_type=F32)
    for part in range(parts):
        r0 = part * pr
        idx_out = jnp.zeros((pr, LANES), jnp.int32)
        for r in range(TOP_EXPERTS):
            arg = all_args[part][r]
            rank = jnp.sum(jnp.where(lane == arg, before[r0:r0 + pr], 0.0), axis=-1, keepdims=True)
            idx_out = jnp.where(lane == r, arg, idx_out)
            idx_out = jnp.where(lane == TOP_EXPERTS + r, rank.astype(jnp.int32), idx_out)
        idx_ref[r0:r0 + pr, :] = idx_out
    counts = jnp.sum(member, axis=0, keepdims=True).astype(jnp.int32)
    cnt_ref[0] = jnp.broadcast_to(counts, (8, LANES))


def _out_proj(x2, y_a, y_b, w_o, g_ffn, w_router, b_router):
    n, d = x2.shape
    da = y_a.shape[1]
    wa = w_o[:da].astype(BF16)
    wb = w_o[da:].astype(BF16)
    wr = jnp.concatenate([w_router, jnp.zeros((d, LANES - N_EXPERTS), w_router.dtype)], 1).astype(BF16)
    br = jnp.concatenate([b_router, jnp.zeros((LANES - N_EXPERTS,), F32)]).reshape(1, LANES)
    tm = min(PROJ_ROWS, n)
    row = lambda width: pl.BlockSpec((tm, width), lambda i: (i, 0))
    full = lambda a: pl.BlockSpec(a.shape, lambda i: (0, 0))
    return pl.pallas_call(
        _out_proj_kernel,
        out_shape=(jax.ShapeDtypeStruct((n, d), F32), jax.ShapeDtypeStruct((SLAB * n, LANES), jnp.uint32),
                   jax.ShapeDtypeStruct((n, LANES), jnp.int32), jax.ShapeDtypeStruct((n, LANES), F32),
                   jax.ShapeDtypeStruct((n // tm, 8, LANES), jnp.int32)),
        grid=(n // tm,),
        in_specs=[row(d), row(da), row(y_b.shape[1]), full(wa), full(wb), full(g_ffn), full(wr), full(br)],
        out_specs=(row(d), pl.BlockSpec((SLAB * tm, LANES), lambda i: (i, 0)), row(LANES), row(LANES),
                   pl.BlockSpec((1, 8, LANES), lambda i: (i, 0, 0))),
        compiler_params=_params("arbitrary"),
        name="out_proj",
    )(x2, y_a, y_b, wa, wb, g_ffn, wr, br)


def _moe_plan(top_idx, counts, tile_rows):
    n = top_idx.shape[0]
    rows = MOE_ROWS
    n_blocks = -(-(n * TOP_EXPERTS + N_EXPERTS * (rows - 1)) // rows)
    total = jnp.sum(counts, axis=0)
    padded = ((total + rows - 1) // rows) * rows
    pend = jnp.cumsum(padded)
    base = (pend - padded)[None, :] + jnp.cumsum(counts, axis=0) - counts
    expert = top_idx[:, :TOP_EXPERTS]
    rank = top_idx[:, TOP_EXPERTS:2 * TOP_EXPERTS]
    base_tok = jnp.repeat(base, tile_rows, axis=0)
    hit = expert[:, :, None] == jnp.arange(N_EXPERTS, dtype=jnp.int32)[None, None, :]
    dest = jnp.sum(jnp.where(hit, base_tok[:, None, :], 0), axis=-1) + rank
    block_start = jnp.arange(n_blocks, dtype=jnp.int32) * rows
    block_expert = jnp.minimum(jnp.sum((block_start[:, None] >= pend[None, :]).astype(jnp.int32), axis=1),
                               N_EXPERTS - 1).astype(jnp.int32)
    n_used = (pend[-1] // rows).astype(jnp.int32).reshape(1)
    return dest.astype(jnp.int32).reshape(-1), block_expert, n_used, n_blocks


def _dispatch_kernel(dest_ref, h_ref, xs_in, xs_hbm, sem):
    del xs_in
    per_step = dest_ref.shape[2]
    tm = per_step // TOP_EXPERTS

    def issue(g, _):
        first = g * (ISSUE_GROUP * TOP_EXPERTS)
        slots = [dest_ref[0, 0, first + j] for j in range(ISSUE_GROUP * TOP_EXPERTS)]
        for j, slot in enumerate(slots):
            t = g * ISSUE_GROUP + j // TOP_EXPERTS
            src = h_ref.at[pl.ds(pl.multiple_of(SLAB * t, SLAB), SLAB)]
            dst = xs_hbm.at[pl.ds(pl.multiple_of(SLAB * slot, SLAB), SLAB)]
            pltpu.make_async_copy(src, dst, sem).start(priority=j % 2)
        return 0

    lax.fori_loop(0, tm // ISSUE_GROUP, issue, 0)
    whole = xs_hbm.at[pl.ds(0, SLAB * per_step)]
    pltpu.make_async_copy(whole, whole, sem).wait()


def _moe_dispatch(hp, dest, m):
    n = hp.shape[0] // SLAB
    tm = min(PROJ_ROWS, n)
    per_step = tm * TOP_EXPERTS
    return pl.pallas_call(
        _dispatch_kernel,
        out_shape=jax.ShapeDtypeStruct((SLAB * m, LANES), jnp.uint32),
        grid=(n // tm,),
        in_specs=[pl.BlockSpec((1, 1, per_step), lambda i: (i, 0, 0), memory_space=pltpu.SMEM),
                  pl.BlockSpec((SLAB * tm, LANES), lambda i: (i, 0)), pl.BlockSpec(memory_space=pl.ANY)],
        out_specs=pl.BlockSpec(memory_space=pl.ANY),
        scratch_shapes=[pltpu.SemaphoreType.DMA],
        input_output_aliases={2: 0},
        compiler_params=_params("arbitrary"),
        name="moe_dispatch",
    )(dest.reshape(n // tm, 1, per_step), hp, jnp.zeros((SLAB * m, LANES), jnp.uint32))


def _moe_kernel(bexp_ref, nused_ref, x_ref, wgu_ref, bgu_ref, wd_ref, bd_ref, y_ref, wgu_bf, wd_bf):
    i = pl.program_id(0)
    rows = x_ref.shape[0] // SLAB
    half = SLAB * LANES
    f = wd_ref.shape[1]

    @pl.when(jnp.logical_or(i == 0, bexp_ref[i] != bexp_ref[jnp.maximum(i - 1, 0)]))
    def _():
        wgu_bf[...] = wgu_ref[0].astype(BF16)
        wd_bf[...] = wd_ref[0].astype(BF16)

    @pl.when(i < nused_ref[0])
    def _():
        lo, hi = _unpack_halves(_load_slab(x_ref, 0, rows))
        gu = (jnp.dot(lo.astype(BF16), wgu_bf[:half, :], preferred_element_type=F32)
              + jnp.dot(hi.astype(BF16), wgu_bf[half:, :], preferred_element_type=F32)
              + bgu_ref[0])
        glu = jnp.minimum(gu[:, :f], SWIGLU_LIMIT)
        lin = jnp.clip(gu[:, f:], -SWIGLU_LIMIT, SWIGLU_LIMIT)
        act = glu * jax.nn.sigmoid(SWIGLU_ALPHA * glu) * (lin + 1.0)
        y = jnp.dot(act.astype(BF16), wd_bf[...], preferred_element_type=F32) + bd_ref[0]
        _store_slab(y_ref, 0, _pack_halves(y))

    @pl.when(i >= nused_ref[0])
    def _():
        y_ref[...] = jnp.zeros(y_ref.shape, y_ref.dtype)


def _moe_experts(xs, block_expert, n_used, w_gu, b_gu, w_down, b_down):
    m = xs.shape[0] // SLAB
    d = 2 * SLAB * LANES
    f = w_down.shape[1]
    rows = MOE_ROWS
    bgu = b_gu.reshape(N_EXPERTS, 1, 2 * f)
    bd = b_down.reshape(N_EXPERTS, 1, d)
    slots = pl.BlockSpec((SLAB * rows, LANES), lambda i, be, nu: (i, 0))
    by_expert = lambda shape: pl.BlockSpec((1,) + shape, lambda i, be, nu: (be[i], 0, 0))
    return pl.pallas_call(
        _moe_kernel,
        out_shape=jax.ShapeDtypeStruct((SLAB * m, LANES), jnp.uint32),
        grid_spec=pltpu.PrefetchScalarGridSpec(
            num_scalar_prefetch=2,
            grid=(m // rows,),
            in_specs=[slots, by_expert((d, 2 * f)), by_expert((1, 2 * f)),
                      by_expert((f, d)), by_expert((1, d))],
            out_specs=slots,
            scratch_shapes=[pltpu.VMEM((d, 2 * f), BF16), pltpu.VMEM((f, d), BF16)]),
        compiler_params=pltpu.CompilerParams(dimension_semantics=("arbitrary",),
                                             vmem_limit_bytes=MOE_VMEM_LIMIT),
        name="moe_experts",
    )(block_expert, n_used, xs, w_gu, bgu, w_down, bd)


def _combine_kernel(dcur_ref, dnext_ref, y_hbm, x_ref, gate_ref, o_ref, ybuf, sems):
    i = pl.program_id(0)
    nsteps = pl.num_programs(0)
    tm = x_ref.shape[0]
    half = SLAB * LANES
    slot = lax.rem(i, 2)

    def issue(dest_ref, buf):
        def body(g, _):
            first = g * (ISSUE_GROUP * TOP_EXPERTS)
            slots = [dest_ref[0, 0, first + j] for j in range(ISSUE_GROUP * TOP_EXPERTS)]
            for j, slot in enumerate(slots):
                t = g * ISSUE_GROUP + j // TOP_EXPERTS
                src = pl.multiple_of(SLAB * slot, SLAB)
                dst = pl.multiple_of(SLAB * ((j % TOP_EXPERTS) * tm + t), SLAB)
                pltpu.make_async_copy(y_hbm.at[pl.ds(src, SLAB)], ybuf.at[buf, pl.ds(dst, SLAB)],
                                      sems.at[buf]).start(priority=j % 2)
            return 0
        lax.fori_loop(0, tm // ISSUE_GROUP, body, 0)

    @pl.when(i == 0)
    def _():
        issue(dcur_ref, 0)

    @pl.when(i + 1 < nsteps)
    def _():
        issue(dnext_ref, 1 - slot)

    pltpu.make_async_copy(y_hbm.at[pl.ds(0, SLAB * TOP_EXPERTS * tm)], ybuf.at[slot], sems.at[slot]).wait()
    lo = x_ref[:, :half]
    hi = x_ref[:, half:]
    for k in range(TOP_EXPERTS):
        a, b = _unpack_halves(_load_slab(ybuf.at[slot], k * tm, tm))
        g = gate_ref[:, k:k + 1]
        lo = lo + a * g
        hi = hi + b * g
    o_ref[:, :half] = lo
    o_ref[:, half:] = hi


def _moe_combine(x1, y_rows, dest, gates):
    n, d = x1.shape
    tm = min(COMBINE_ROWS, n)
    nt = n // tm
    per_step = tm * TOP_EXPERTS
    dest3 = dest.reshape(nt, 1, per_step)
    return pl.pallas_call(
        _combine_kernel,
        out_shape=jax.ShapeDtypeStruct((n, d), F32),
        grid=(nt,),
        in_specs=[pl.BlockSpec((1, 1, per_step), lambda i: (i, 0, 0), memory_space=pltpu.SMEM),
                  pl.BlockSpec((1, 1, per_step), lambda i: (jnp.minimum(i + 1, nt - 1), 0, 0),
                               memory_space=pltpu.SMEM),
                  pl.BlockSpec(memory_space=pl.ANY),
                  pl.BlockSpec((tm, d), lambda i: (i, 0)),
                  pl.BlockSpec((tm, LANES), lambda i: (i, 0))],
        out_specs=pl.BlockSpec((tm, d), lambda i: (i, 0)),
        scratch_shapes=[pltpu.VMEM((2, SLAB * per_step, LANES), jnp.uint32), pltpu.SemaphoreType.DMA((2,))],
        compiler_params=_params("arbitrary"),
        name="moe_combine",
    )(dest3, dest3, y_rows, x1, gates)


def kernel(x, positions, g_attn, w_in, g_qa, g_ka, g_idx_k, lambda_q1, lambda_k1, lambda_q2, lambda_k2, g_qb, g_kb, g_subln, w_o, g_ffn, w_router, b_router, w_gu, b_gu, w_down, b_down):
    b, s, d = x.shape
    n = b * s
    k_sel = min(TOPK_KEYS, s // 4)
    c_tab, s_tab = _rope_tables(positions)
    for layer in range(g_attn.shape[0]):
        lam_init = 0.8 - 0.6 * math.exp(-0.3 * layer)
        x2 = x.reshape(n, d)
        aq, iq, bq, bk, bv, kka, kki, v_t, iw_t = _in_proj(
            x2, g_attn[layer].reshape(1, d), w_in[layer], c_tab, s_tab,
            g_qa[layer], g_ka[layer], g_idx_k[layer], g_qb[layer], g_kb[layer])
        seq = lambda t: t.reshape(b, s, t.shape[-1])
        y_a = _dsa_attn(seq(aq), seq(iq), seq(kka), seq(kki), v_t, iw_t, k_sel)
        lam = (jnp.exp(jnp.sum(lambda_q1[layer].astype(F32) * lambda_k1[layer].astype(F32)))
               - jnp.exp(jnp.sum(lambda_q2[layer].astype(F32) * lambda_k2[layer].astype(F32)))
               + lam_init).reshape(1)
        y_b = _diff_attn(seq(bq), seq(bk), seq(bv), lam, g_subln[layer].reshape(1, B_V_DIM), lam_init)
        x1, hp, top_idx, gates, counts = _out_proj(
            x2, y_a.reshape(n, -1), y_b.reshape(n, -1), w_o[layer], g_ffn[layer].reshape(1, d),
            w_router[layer], b_router[layer])
        dest, block_expert, n_used, n_blocks = _moe_plan(
            top_idx, counts[:, 0, :N_EXPERTS], n // counts.shape[0])
        xs = _moe_dispatch(hp, dest, n_blocks * MOE_ROWS)
        y_rows = _moe_experts(xs, block_expert, n_used, w_gu[layer], b_gu[layer], w_down[layer], b_down[layer])
        x = _moe_combine(x1, y_rows, dest, gates).reshape(b, s, d)
    return x
```

```python
import functools
import math

import jax
import jax.numpy as jnp
import numpy as np
from jax import lax
from jax.experimental import pallas as pl
from jax.experimental.pallas import tpu as pltpu

CHUNK = 64
ROPE_THETA = 500000.0
NORM_EPS = 1e-6
A_HEADS = 8
HEAD_DIM = 64
IDX_HEADS = 8
TOPK_KEYS = 256
B_HEADS = 4
B_V_DIM = 128
ROT_DIM = HEAD_DIM // 4
ROT_HALF = ROT_DIM // 2
N_EXPERTS = 32
TOP_EXPERTS = 4
SWIGLU_LIMIT = 7.0
SWIGLU_ALPHA = 1.702

LANES = 128
VMEM_LIMIT = 48 * 1024 * 1024
MOE_VMEM_LIMIT = 56 * 1024 * 1024

PROJ_ROWS = 512
DIFF_Q = 256
DIFF_K = 512
DSA_Q = 256
DSA_K = 256
MOE_ROWS = 512
COMBINE_ROWS = 256
ISSUE_GROUP = 4

MASKED = -1e30
LOGIT_SCALE = HEAD_DIM ** -0.5 * math.log2(math.e)
INT_MIN = -(2 ** 31)

BF16 = jnp.bfloat16
F32 = jnp.float32
NT_DIMS = (((1,), (1,)), ((), ()))


def _params(*sem):
    return pltpu.CompilerParams(dimension_semantics=sem, vmem_limit_bytes=VMEM_LIMIT)


def _rope_kernel(invf_ref, pos_ref, cos_ref, sin_ref):
    f = pl.program_id(0)
    ang = pos_ref[...].astype(F32) * invf_ref[f]
    cos_ref[0] = jnp.cos(ang)
    sin_ref[0] = jnp.sin(ang)


def _rope_tables(positions):
    b, s = positions.shape
    inv_freq = ROPE_THETA ** (-jnp.arange(0, ROT_DIM, 2, dtype=F32) / ROT_DIM)
    cos_t, sin_t = pl.pallas_call(
        _rope_kernel,
        out_shape=(jax.ShapeDtypeStruct((ROT_HALF, b, s), F32),) * 2,
        grid=(ROT_HALF,),
        in_specs=[pl.BlockSpec(memory_space=pltpu.SMEM),
                  pl.BlockSpec((b, s), lambda f: (0, 0))],
        out_specs=(pl.BlockSpec((1, b, s), lambda f: (f, 0, 0)),) * 2,
        compiler_params=_params("arbitrary"),
        name="rope_tables",
    )(inv_freq, positions)
    n = b * s
    cos8 = jnp.transpose(cos_t, (1, 2, 0)).reshape(n, ROT_HALF)
    sin8 = jnp.transpose(sin_t, (1, 2, 0)).reshape(n, ROT_HALF)
    ones = jnp.ones((n, HEAD_DIM - ROT_DIM), F32)
    zeros = jnp.zeros((n, HEAD_DIM - ROT_DIM), F32)
    c_tab = jnp.concatenate([cos8, cos8, ones], axis=1)
    s_tab = jnp.concatenate([-sin8, sin8, zeros], axis=1)
    return jnp.tile(c_tab, (1, 2)), jnp.tile(s_tab, (1, 2))


def _same_head_ones():
    r = lax.broadcasted_iota(jnp.int32, (LANES, LANES), 0) // HEAD_DIM
    c = lax.broadcasted_iota(jnp.int32, (LANES, LANES), 1) // HEAD_DIM
    return jnp.where(r == c, 1.0, 0.0).astype(BF16)


def _segment_rms(y, gain, same_head):
    ss = jnp.dot((y * y).astype(BF16), same_head, preferred_element_type=F32)
    return y * lax.rsqrt(ss * (1.0 / HEAD_DIM) + NORM_EPS) * gain


def _rope128(y, c_tab, s_tab):
    lane = lax.broadcasted_iota(jnp.int32, (1, LANES), 1)
    first = (lane % HEAD_DIM) < ROT_HALF
    up = pltpu.roll(y, LANES - ROT_HALF, 1)
    down = pltpu.roll(y, ROT_HALF, 1)
    return y * c_tab + jnp.where(first, up, down) * s_tab


def _in_proj_kernel(x_ref, g_ref, w_ref, c_ref, s_ref, gains_ref,
                    aq_ref, iq_ref, bq_ref, bk_ref, bv_ref, kka_ref, kki_ref, vt_ref, iw_ref):
    x = x_ref[...]
    ms = jnp.mean(x * x, axis=-1, keepdims=True)
    h = (x * lax.rsqrt(ms + NORM_EPS) * g_ref[...]).astype(BF16)
    c_tab = c_ref[...]
    s_tab = s_ref[...]
    wide = 4 * LANES
    same_head = _same_head_ones()

    def group(col, out_ref, gain_row, rope):
        acc = jnp.dot(h, w_ref[:, col:col + wide], preferred_element_type=F32)
        for t in range(4):
            y = acc[:, t * LANES:(t + 1) * LANES]
            if gain_row is not None:
                y = _segment_rms(y, gains_ref[gain_row:gain_row + 1, :], same_head)
            if rope:
                y = _rope128(y, c_tab, s_tab)
            out_ref[:, t * LANES:(t + 1) * LANES] = y.astype(out_ref.dtype)

    group(0 * wide, aq_ref, 0, True)
    group(1 * wide, iq_ref, None, True)
    group(2 * wide, bq_ref, 1, True)
    group(3 * wide, bk_ref, 2, True)
    group(4 * wide, bv_ref, None, False)
    acc = jnp.dot(h, w_ref[:, 5 * wide:6 * wide], preferred_element_type=F32)
    ka = _rope128(_segment_rms(acc[:, 0:LANES], gains_ref[3:4, :], same_head), c_tab, s_tab)
    ki = _rope128(_segment_rms(acc[:, LANES:2 * LANES], gains_ref[4:5, :], same_head), c_tab, s_tab)
    kka_ref[...] = ka.astype(BF16)
    kki_ref[...] = ki.astype(BF16)
    tk = vt_ref.shape[2]
    for c in range(vt_ref.shape[0]):
        v_t = acc[c * tk:(c + 1) * tk, 2 * LANES:3 * LANES].T
        vt_ref[c] = v_t[:HEAD_DIM].astype(BF16)
    iw_t = (acc[:, 3 * LANES:4 * LANES] * ((IDX_HEADS * HEAD_DIM) ** -0.5)).T
    iw_ref[...] = iw_t[:IDX_HEADS]


def _in_proj(x2, g_attn, w_in, c_tab, s_tab, g_qa, g_ka, g_idx_k, g_qb, g_kb):
    n, d = x2.shape
    ha = A_HEADS * HEAD_DIM
    o = np.cumsum([0, ha, HEAD_DIM, HEAD_DIM, IDX_HEADS * HEAD_DIM, HEAD_DIM, IDX_HEADS,
                   B_HEADS * 2 * HEAD_DIM, B_HEADS * 2 * HEAD_DIM, B_HEADS * B_V_DIM])
    aq, ak, av, iq, ik, iw, bq, bk, bv = [w_in[:, o[i]:o[i + 1]] for i in range(9)]
    pad = jnp.zeros((d, LANES - IDX_HEADS), w_in.dtype)
    w = jnp.concatenate([aq, iq, bq, bk, bv, ak, ak, ik, ik, av, av, iw, pad], axis=1).astype(BF16)
    two = lambda g: jnp.tile(g.reshape(1, HEAD_DIM), (1, 2))
    gains = jnp.concatenate([two(g_qa) * LOGIT_SCALE, two(g_qb) * LOGIT_SCALE,
                             two(g_kb), two(g_ka), two(g_idx_k),
                             jnp.zeros((3, LANES), F32)], axis=0)
    tm = min(PROJ_ROWS, n)
    tk = min(DSA_K, tm)
    wide = 4 * LANES
    row = lambda width: pl.BlockSpec((tm, width), lambda i: (i, 0))
    full = lambda a: pl.BlockSpec(a.shape, lambda i: (0, 0))
    outs = pl.pallas_call(
        _in_proj_kernel,
        out_shape=(jax.ShapeDtypeStruct((n, wide), BF16),) * 5
        + (jax.ShapeDtypeStruct((n, LANES), BF16),) * 2
        + (jax.ShapeDtypeStruct((n // tk, HEAD_DIM, tk), BF16),
           jax.ShapeDtypeStruct((IDX_HEADS, n), F32)),
        grid=(n // tm,),
        in_specs=[row(d), full(g_attn), full(w), row(LANES), row(LANES), full(gains)],
        out_specs=(row(wide),) * 5 + (row(LANES),) * 2
        + (pl.BlockSpec((tm // tk, HEAD_DIM, tk), lambda i: (i, 0, 0)),
           pl.BlockSpec((IDX_HEADS, tm), lambda i: (0, i))),
        compiler_params=_params("arbitrary"),
        name="in_proj",
    )(x2, g_attn, w, c_tab, s_tab, gains)
    return outs


def _diff_attn_kernel(lam_ref, q_ref, k_ref, v_ref, g_ref, o_ref, vt_ref, *, lam_init):
    i = pl.program_id(1)
    tq = q_ref.shape[1]
    tk = vt_ref.shape[3]
    lam = lam_ref[0]
    nkb = (i * tq + tq + tk - 1) // tk
    lane = lax.broadcasted_iota(jnp.int32, (1, LANES), 1)
    lo = lane < HEAD_DIM
    k_chunk0 = lax.broadcasted_iota(jnp.int32, (tk, tq), 0) // CHUNK
    q_chunk = (i * tq + lax.broadcasted_iota(jnp.int32, (tk, tq), 1)) // CHUNK

    @pl.when(i == 0)
    def _():
        for hd in range(B_HEADS):
            for c in range(vt_ref.shape[1]):
                blk = v_ref[0, c * tk:(c + 1) * tk, hd * LANES:(hd + 1) * LANES]
                vt_ref[hd, c] = blk.astype(F32).T.astype(BF16)

    qs = []
    for hd in range(B_HEADS):
        qh = q_ref[0, :, hd * LANES:(hd + 1) * LANES]
        zero = jnp.zeros_like(qh)
        qs.append(jnp.concatenate([jnp.where(lo, qh, zero), jnp.where(lo, zero, qh)], axis=0))

    def body(j, carry):
        start = pl.multiple_of(j * tk, tk)
        ok = k_chunk0 + j * (tk // CHUNK) <= q_chunk
        out = []
        for hd in range(B_HEADS):
            m, l, acc = carry[hd]
            kj = k_ref[0, pl.ds(start, tk), hd * LANES:(hd + 1) * LANES]
            st = lax.dot_general(kj, qs[hd], NT_DIMS, preferred_element_type=F32)
            m_out, a_out, l_out, p_out = [], [], [], []
            for c in range(2 * tq // LANES):
                cols = slice(c * LANES, (c + 1) * LANES)
                qcols = slice((c * LANES) % tq, (c * LANES) % tq + LANES)
                s = jnp.where(ok[:, qcols], st[:, cols], MASKED)
                m_new = jnp.maximum(m[:, cols], jnp.max(s, axis=0, keepdims=True))
                alpha = jnp.exp2(m[:, cols] - m_new)
                p = jnp.exp2(s - m_new)
                m_out.append(m_new)
                a_out.append(alpha)
                l_out.append(alpha * l[:, cols] + jnp.sum(p, axis=0, keepdims=True))
                p_out.append(p.astype(BF16))
            pv = jnp.dot(vt_ref[hd, j], jnp.concatenate(p_out, axis=1), preferred_element_type=F32)
            out.append((jnp.concatenate(m_out, axis=1), jnp.concatenate(l_out, axis=1),
                        jnp.concatenate(a_out, axis=1) * acc + pv))
        return tuple(out)

    init = tuple((jnp.full((1, 2 * tq), MASKED, F32), jnp.zeros((1, 2 * tq), F32),
                  jnp.zeros((B_V_DIM, 2 * tq), F32)) for _ in range(B_HEADS))
    final = lax.fori_loop(0, nkb, body, init)
    for hd in range(B_HEADS):
        cols = slice(hd * LANES, (hd + 1) * LANES)
        _, l, acc = final[hd]
        o = acc / l
        o = o[:, :tq] - lam * o[:, tq:]
        ms = jnp.mean(o * o, axis=0, keepdims=True)
        o = o * lax.rsqrt(ms + NORM_EPS) * g_ref[...] * (1.0 - lam_init)
        o_ref[0, :, cols] = o.T.astype(o_ref.dtype)


def _diff_attn(bq, bk, bv, lam, g_subln, lam_init):
    b, s, w = bq.shape
    tq = min(DIFF_Q, s)
    tk = min(DIFF_K, s)
    kv = pl.BlockSpec((1, s, w), lambda bi, i: (bi, 0, 0))
    qo = pl.BlockSpec((1, tq, w), lambda bi, i: (bi, i, 0))
    return pl.pallas_call(
        functools.partial(_diff_attn_kernel, lam_init=lam_init),
        out_shape=jax.ShapeDtypeStruct((b, s, w), BF16),
        grid=(b, s // tq),
        in_specs=[pl.BlockSpec(memory_space=pltpu.SMEM), qo, kv, kv,
                  pl.BlockSpec((B_V_DIM, 1), lambda bi, i: (0, 0))],
        out_specs=qo,
        scratch_shapes=[pltpu.VMEM((B_HEADS, s // tk, B_V_DIM, tk), BF16)],
        compiler_params=_params("arbitrary", "arbitrary"),
        name="diff_attn",
    )(lam, bq, bk, bv, g_subln.reshape(B_V_DIM, 1))


def _stack_heads(src_ref, dst_ref):
    tq = src_ref.shape[1]
    lane = lax.broadcasted_iota(jnp.int32, (1, LANES), 1)
    lo = lane < HEAD_DIM
    for hd in range(A_HEADS):
        blk = src_ref[0, :, (hd // 2) * LANES:(hd // 2 + 1) * LANES]
        keep = lo if hd % 2 == 0 else jnp.logical_not(lo)
        dst_ref[hd * tq:(hd + 1) * tq, :] = jnp.where(keep, blk, jnp.zeros_like(blk))


def _dsa_attn_kernel(aq_ref, iq_ref, kka_ref, kki_ref, vt_ref, iw_ref, o_ref,
                     qa_ref, qi_ref, key_ref, *, k_sel):
    i = pl.program_id(1)
    tq = aq_ref.shape[1]
    tk = key_ref.shape[1]
    nsel = (i * tq + tq + tk - 1) // tk
    _stack_heads(aq_ref, qa_ref)
    _stack_heads(iq_ref, qi_ref)
    sub = lax.broadcasted_iota(jnp.int32, (tk, tq), 0)
    k_chunk0 = sub // CHUNK
    q_chunk = (i * tq + lax.broadcasted_iota(jnp.int32, (tk, tq), 1)) // CHUNK

    def score_block(j, _):
        start = pl.multiple_of(j * tk, tk)
        kj = kki_ref[0, pl.ds(start, tk), :]
        st = lax.dot_general(kj, qi_ref[...], NT_DIMS, preferred_element_type=F32)
        score = jnp.zeros((tk, tq), F32)
        for hd in range(IDX_HEADS):
            rel = jnp.maximum(st[:, hd * tq:(hd + 1) * tq], 0.0)
            score = score + rel * iw_ref[hd:hd + 1, :]
        bits = pltpu.bitcast(score, jnp.int32)
        key = jnp.where(bits < 0, jnp.int32(INT_MIN) - bits, bits)
        key_ref[j] = jnp.where(k_chunk0 + j * (tk // CHUNK) > q_chunk, jnp.int32(INT_MIN), key)
        return 0

    lax.fori_loop(0, nsel, score_block, 0)

    def count(pred):
        def body(j, acc):
            hit = jnp.where(pred(key_ref[j], j), 1, 0).astype(jnp.int32)
            return acc + jnp.sum(hit.reshape(tk // 8, 8, tq), axis=0)
        acc = lax.fori_loop(0, nsel, body, jnp.zeros((8, tq), jnp.int32))
        return jnp.sum(acc, axis=0, keepdims=True)

    def search(groups, start):
        if groups * tk <= k_sel:
            return start

        def search_bit(t, thr):
            cand = thr + lax.shift_left(jnp.int32(1), 31 - t)
            acc = jnp.zeros((8, tq), jnp.int32)
            for j in range(groups):
                hit = jnp.where(key_ref[j] >= cand, 1, 0).astype(jnp.int32)
                acc = acc + jnp.sum(hit.reshape(tk // 8, 8, tq), axis=0)
            cnt = jnp.sum(acc, axis=0, keepdims=True)
            return jnp.where(cnt >= k_sel, cand, thr)
        return lax.fori_loop(0, 32, search_bit, start)

    thr = lax.switch(nsel - 1, [functools.partial(search, g + 1) for g in range(key_ref.shape[0])],
                     jnp.full((1, tq), INT_MIN, jnp.int32))
    n_gt = count(lambda key, j: key > thr)
    n_eq = count(lambda key, j: key == thr)
    need = k_sel - n_gt
    tie_break = jnp.logical_and(thr > INT_MIN, n_eq > need)

    def tie_search(_):
        def bit(t, cut):
            cand = cut + lax.shift_left(jnp.int32(1), 15 - t)
            below = count(lambda key, j: jnp.logical_and(key == thr, sub + j * tk < cand))
            return jnp.where(below < need, cand, cut)
        return lax.fori_loop(0, 16, bit, jnp.zeros((1, tq), jnp.int32))

    any_tie = jnp.max(jnp.where(tie_break, 1, 0)) > 0
    cut = lax.cond(any_tie, tie_search, lambda _: jnp.full((1, tq), 2 ** 30, jnp.int32), 0)
    cut = jnp.where(tie_break, cut, 2 ** 30)

    rows = A_HEADS * tq

    def attn_block(j, carry):
        m, l, acc = carry
        start = pl.multiple_of(j * tk, tk)
        kj = kka_ref[0, pl.ds(start, tk), :]
        key = key_ref[j]
        sel = jnp.logical_or(key > thr, jnp.logical_and(key == thr, sub + j * tk <= cut))
        sel = jnp.logical_and(sel, key > INT_MIN)
        st = lax.dot_general(kj, qa_ref[...], NT_DIMS, preferred_element_type=F32)
        m_out, a_out, l_out, p_out = [], [], [], []
        for c in range(rows // LANES):
            cols = slice(c * LANES, (c + 1) * LANES)
            qcols = slice((c * LANES) % tq, (c * LANES) % tq + LANES)
            s = jnp.where(sel[:, qcols], st[:, cols], MASKED)
            m_new = jnp.maximum(m[:, cols], jnp.max(s, axis=0, keepdims=True))
            alpha = jnp.exp2(m[:, cols] - m_new)
            p = jnp.exp2(s - m_new)
            m_out.append(m_new)
            a_out.append(alpha)
            l_out.append(alpha * l[:, cols] + jnp.sum(p, axis=0, keepdims=True))
            p_out.append(p.astype(BF16))
        alpha = jnp.concatenate(a_out, axis=1)
        pv = jnp.dot(vt_ref[j], jnp.concatenate(p_out, axis=1), preferred_element_type=F32)
        return jnp.concatenate(m_out, axis=1), jnp.concatenate(l_out, axis=1), alpha * acc + pv

    init = (jnp.full((1, rows), MASKED, F32), jnp.zeros((1, rows), F32), jnp.zeros((HEAD_DIM, rows), F32))
    _, l, acc = lax.fori_loop(0, nsel, attn_block, init)
    out_t = acc / l
    for g in range(A_HEADS // 2):
        pair = out_t[:, 2 * g * tq:(2 * g + 2) * tq]
        pair = jnp.concatenate([pair[:, :tq], pair[:, tq:]], axis=0)
        o_ref[0, :, g * LANES:(g + 1) * LANES] = pair.T.astype(o_ref.dtype)


def _dsa_attn(aq, iq, kka, kki, v_t, iw_t, k_sel):
    b, s, w = aq.shape
    tq = min(DSA_Q, s)
    tk = v_t.shape[2]
    nq = s // tq
    qo = pl.BlockSpec((1, tq, w), lambda bi, i: (bi, i, 0))
    kv = pl.BlockSpec((1, s, LANES), lambda bi, i: (bi, 0, 0))
    return pl.pallas_call(
        functools.partial(_dsa_attn_kernel, k_sel=k_sel),
        out_shape=jax.ShapeDtypeStruct((b, s, w), BF16),
        grid=(b, nq),
        in_specs=[qo, qo, kv, kv,
                  pl.BlockSpec((s // tk, HEAD_DIM, tk), lambda bi, i: (bi, 0, 0)),
                  pl.BlockSpec((IDX_HEADS, tq), lambda bi, i: (0, bi * nq + i))],
        out_specs=qo,
        scratch_shapes=[pltpu.VMEM((A_HEADS * tq, LANES), BF16),
                        pltpu.VMEM((IDX_HEADS * tq, LANES), BF16),
                        pltpu.VMEM((s // tk, tk, tq), jnp.int32)],
        compiler_params=_params("arbitrary", "arbitrary"),
        name="dsa_attn",
    )(aq, iq, kka, kki, v_t, iw_t)


def _pack_halves(y):
    w = y.shape[1] // 2
    r = y.astype(BF16).astype(F32)
    lo = pltpu.bitcast(r[:, :w], jnp.uint32)
    hi = pltpu.bitcast(r[:, w:], jnp.uint32)
    return jnp.bitwise_or(jnp.bitwise_and(hi, jnp.uint32(0xFFFF0000)), lax.shift_right_logical(lo, jnp.uint32(16)))


def _unpack_halves(u):
    lo = pltpu.bitcast(lax.shift_left(u, jnp.uint32(16)), F32)
    hi = pltpu.bitcast(jnp.bitwise_and(u, jnp.uint32(0xFFFF0000)), F32)
    return lo, hi


SLAB = 4


def _store_slab(ref, first, packed):
    rows = packed.shape[0]
    for c in range(SLAB):
        ref[pl.ds(SLAB * first + c, rows, stride=SLAB), :] = packed[:, c * LANES:(c + 1) * LANES]


def _load_slab(ref, first, rows):
    return jnp.concatenate([ref[pl.ds(SLAB * first + c, rows, stride=SLAB), :] for c in range(SLAB)], axis=1)


def _out_proj_kernel(x_ref, ya_ref, yb_ref, wa_ref, wb_ref, g_ref, wr_ref, br_ref,
                     x1_ref, hp_ref, idx_ref, gate_ref, cnt_ref):
    tm = x_ref.shape[0]
    parts = 2 if tm % 16 == 0 else 1
    pr = tm // parts
    lane = lax.broadcasted_iota(jnp.int32, (pr, LANES), 1)
    members, all_args = [], []
    for part in range(parts):
        r0 = part * pr
        x1 = (x_ref[r0:r0 + pr, :]
              + jnp.dot(ya_ref[r0:r0 + pr, :], wa_ref[...], preferred_element_type=F32)
              + jnp.dot(yb_ref[r0:r0 + pr, :], wb_ref[...], preferred_element_type=F32))
        x1_ref[r0:r0 + pr, :] = x1
        ms = jnp.mean(x1 * x1, axis=-1, keepdims=True)
        h = x1 * lax.rsqrt(ms + NORM_EPS) * g_ref[...]
        _store_slab(hp_ref, r0, _pack_halves(h))
        logits = jnp.dot(h.astype(BF16), wr_ref[...], preferred_element_type=F32) + br_ref[...]
        work = jnp.where(lane < N_EXPERTS, logits, -jnp.inf)
        val_out = jnp.full(logits.shape, -jnp.inf, F32)
        member = jnp.zeros(logits.shape, F32)
        args = []
        for r in range(TOP_EXPERTS):
            top = jnp.max(work, axis=-1, keepdims=True)
            arg = jnp.min(jnp.where(work == top, lane, LANES), axis=-1, keepdims=True)
            args.append(arg)
            val_out = jnp.where(lane == r, top, val_out)
            member = jnp.where(lane == arg, 1.0, member)
            work = jnp.where(lane == arg, -jnp.inf, work)
        e = jnp.exp(val_out - jnp.max(val_out, axis=-1, keepdims=True))
        gate_ref[r0:r0 + pr, :] = e / jnp.sum(e, axis=-1, keepdims=True)
        members.append(member)
        all_args.append(args)
    member = jnp.concatenate(members, axis=0)
    earlier = jnp.where(lax.broadcasted_iota(jnp.int32, (tm, tm), 1)
                        < lax.broadcasted_iota(jnp.int32, (tm, tm), 0), 1.0, 0.0).astype(BF16)
    before = jnp.dot(earlier, member.astype(BF16), preferred_element_type=F32)
    for part in range(parts):
        r0 = part * pr
        idx_out = jnp.zeros((pr, LANES), jnp.int32)
        for r in range(TOP_EXPERTS):
            arg = all_args[part][r]
            rank = jnp.sum(jnp.where(lane == arg, before[r0:r0 + pr], 0.0), axis=-1, keepdims=True)
            idx_out = jnp.where(lane == r, arg, idx_out)
            idx_out = jnp.where(lane == TOP_EXPERTS + r, rank.astype(jnp.int32), idx_out)
        idx_ref[r0:r0 + pr, :] = idx_out
    counts = jnp.sum(member, axis=0, keepdims=True).astype(jnp.int32)
    cnt_ref[0] = jnp.broadcast_to(counts, (8, LANES))


def _out_proj(x2, y_a, y_b, w_o, g_ffn, w_router, b_router):
    n, d = x2.shape
    da = y_a.shape[1]
    wa = w_o[:da].astype(BF16)
    wb = w_o[da:].astype(BF16)
    wr = jnp.concatenate([w_router, jnp.zeros((d, LANES - N_EXPERTS), w_router.dtype)], 1).astype(BF16)
    br = jnp.concatenate([b_router, jnp.zeros((LANES - N_EXPERTS,), F32)]).reshape(1, LANES)
    tm = min(PROJ_ROWS, n)
    row = lambda width: pl.BlockSpec((tm, width), lambda i: (i, 0))
    full = lambda a: pl.BlockSpec(a.shape, lambda i: (0, 0))
    return pl.pallas_call(
        _out_proj_kernel,
        out_shape=(jax.ShapeDtypeStruct((n, d), F32), jax.ShapeDtypeStruct((SLAB * n, LANES), jnp.uint32),
                   jax.ShapeDtypeStruct((n, LANES), jnp.int32), jax.ShapeDtypeStruct((n, LANES), F32),
                   jax.ShapeDtypeStruct((n // tm, 8, LANES), jnp.int32)),
        grid=(n // tm,),
        in_specs=[row(d), row(da), row(y_b.shape[1]), full(wa), full(wb), full(g_ffn), full(wr), full(br)],
        out_specs=(row(d), pl.BlockSpec((SLAB * tm, LANES), lambda i: (i, 0)), row(LANES), row(LANES),
                   pl.BlockSpec((1, 8, LANES), lambda i: (i, 0, 0))),
        compiler_params=_params("arbitrary"),
        name="out_proj",
    )(x2, y_a, y_b, wa, wb, g_ffn, wr, br)


def _moe_plan(top_idx, counts, tile_rows):
    n = top_idx.shape[0]
    rows = MOE_ROWS
    n_blocks = -(-(n * TOP_EXPERTS + N_EXPERTS * (rows - 1)) // rows)
    total = jnp.sum(counts, axis=0)
    padded = ((total + rows - 1) // rows) * rows
    pend = jnp.cumsum(padded)
    base = (pend - padded)[None, :] + jnp.cumsum(counts, axis=0) - counts
    expert = top_idx[:, :TOP_EXPERTS]
    rank = top_idx[:, TOP_EXPERTS:2 * TOP_EXPERTS]
    base_tok = jnp.repeat(base, tile_rows, axis=0)
    hit = expert[:, :, None] == jnp.arange(N_EXPERTS, dtype=jnp.int32)[None, None, :]
    dest = jnp.sum(jnp.where(hit, base_tok[:, None, :], 0), axis=-1) + rank
    block_start = jnp.arange(n_blocks, dtype=jnp.int32) * rows
    block_expert = jnp.minimum(jnp.sum((block_start[:, None] >= pend[None, :]).astype(jnp.int32), axis=1),
                               N_EXPERTS - 1).astype(jnp.int32)
    n_used = (pend[-1] // rows).astype(jnp.int32).reshape(1)
    return dest.astype(jnp.int32).reshape(-1), block_expert, n_used, n_blocks


def _dispatch_kernel(dest_ref, h_ref, xs_in, xs_hbm, sem):
    del xs_in
    per_step = dest_ref.shape[2]
    tm = per_step // TOP_EXPERTS

    def issue(g, _):
        first = g * (ISSUE_GROUP * TOP_EXPERTS)
        slots = [dest_ref[0, 0, first + j] for j in range(ISSUE_GROUP * TOP_EXPERTS)]
        for j, slot in enumerate(slots):
            t = g * ISSUE_GROUP + j // TOP_EXPERTS
            src = h_ref.at[pl.ds(pl.multiple_of(SLAB * t, SLAB), SLAB)]
            dst = xs_hbm.at[pl.ds(pl.multiple_of(SLAB * slot, SLAB), SLAB)]
            pltpu.make_async_copy(src, dst, sem).start(priority=j % 2)
        return 0

    lax.fori_loop(0, tm // ISSUE_GROUP, issue, 0)
    whole = xs_hbm.at[pl.ds(0, SLAB * per_step)]
    pltpu.make_async_copy(whole, whole, sem).wait()


def _moe_dispatch(hp, dest, m):
    n = hp.shape[0] // SLAB
    tm = min(PROJ_ROWS, n)
    per_step = tm * TOP_EXPERTS
    return pl.pallas_call(
        _dispatch_kernel,
        out_shape=jax.ShapeDtypeStruct((SLAB * m, LANES), jnp.uint32),
        grid=(n // tm,),
        in_specs=[pl.BlockSpec((1, 1, per_step), lambda i: (i, 0, 0), memory_space=pltpu.SMEM),
                  pl.BlockSpec((SLAB * tm, LANES), lambda i: (i, 0)), pl.BlockSpec(memory_space=pl.ANY)],
        out_specs=pl.BlockSpec(memory_space=pl.ANY),
        scratch_shapes=[pltpu.SemaphoreType.DMA],
        input_output_aliases={2: 0},
        compiler_params=_params("arbitrary"),
        name="moe_dispatch",
    )(dest.reshape(n // tm, 1, per_step), hp, jnp.zeros((SLAB * m, LANES), jnp.uint32))


def _moe_kernel(bexp_ref, nused_ref, x_ref, wgu_ref, bgu_ref, wd_ref, bd_ref, y_ref, wgu_bf, wd_bf):
    i = pl.program_id(0)
    rows = x_ref.shape[0] // SLAB
    f = wd_ref.shape[1]

    @pl.when(jnp.logical_or(i == 0, bexp_ref[i] != bexp_ref[jnp.maximum(i - 1, 0)]))
    def _():
        wgu_bf[...] = wgu_ref[0].astype(BF16)
        wd_bf[...] = wd_ref[0].astype(BF16)

    @pl.when(i < nused_ref[0])
    def _():
        lo, hi = _unpack_halves(_load_slab(x_ref, 0, rows))
        x = jnp.concatenate([lo.astype(BF16), hi.astype(BF16)], axis=1)
        gu = jnp.dot(x, wgu_bf[...], preferred_element_type=F32) + bgu_ref[0]
        glu = jnp.minimum(gu[:, :f], SWIGLU_LIMIT)
        lin = jnp.clip(gu[:, f:], -SWIGLU_LIMIT, SWIGLU_LIMIT)
        act = glu * jax.nn.sigmoid(SWIGLU_ALPHA * glu) * (lin + 1.0)
        y = jnp.dot(act.astype(BF16), wd_bf[...], preferred_element_type=F32) + bd_ref[0]
        _store_slab(y_ref, 0, _pack_halves(y))

    @pl.when(i >= nused_ref[0])
    def _():
        y_ref[...] = jnp.zeros(y_ref.shape, y_ref.dtype)


def _moe_experts(xs, block_expert, n_used, w_gu, b_gu, w_down, b_down):
    m = xs.shape[0] // SLAB
    d = 2 * SLAB * LANES
    f = w_down.shape[1]
    rows = MOE_ROWS
    bgu = b_gu.reshape(N_EXPERTS, 1, 2 * f)
    bd = b_down.reshape(N_EXPERTS, 1, d)
    slots = pl.BlockSpec((SLAB * rows, LANES), lambda i, be, nu: (i, 0))
    by_expert = lambda shape: pl.BlockSpec((1,) + shape, lambda i, be, nu: (be[i], 0, 0))
    return pl.pallas_call(
        _moe_kernel,
        out_shape=jax.ShapeDtypeStruct((SLAB * m, LANES), jnp.uint32),
        grid_spec=pltpu.PrefetchScalarGridSpec(
            num_scalar_prefetch=2,
            grid=(m // rows,),
            in_specs=[slots, by_expert((d, 2 * f)), by_expert((1, 2 * f)),
                      by_expert((f, d)), by_expert((1, d))],
            out_specs=slots,
            scratch_shapes=[pltpu.VMEM((d, 2 * f), BF16), pltpu.VMEM((f, d), BF16)]),
        compiler_params=pltpu.CompilerParams(dimension_semantics=("arbitrary",),
                                             vmem_limit_bytes=MOE_VMEM_LIMIT),
        name="moe_experts",
    )(block_expert, n_used, xs, w_gu, bgu, w_down, bd)


def _combine_kernel(dcur_ref, dnext_ref, y_hbm, x_ref, gate_ref, o_ref, ybuf, sems):
    i = pl.program_id(0)
    nsteps = pl.num_programs(0)
    tm = x_ref.shape[0]
    half = SLAB * LANES
    slot = lax.rem(i, 2)

    def issue(dest_ref, buf):
        def body(g, _):
            first = g * (ISSUE_GROUP * TOP_EXPERTS)
            slots = [dest_ref[0, 0, first + j] for j in range(ISSUE_GROUP * TOP_EXPERTS)]
            for j, slot in enumerate(slots):
                t = g * ISSUE_GROUP + j // TOP_EXPERTS
                src = pl.multiple_of(SLAB * slot, SLAB)
                dst = pl.multiple_of(SLAB * ((j % TOP_EXPERTS) * tm + t), SLAB)
                pltpu.make_async_copy(y_hbm.at[pl.ds(src, SLAB)], ybuf.at[buf, pl.ds(dst, SLAB)],
                                      sems.at[buf]).start(priority=j % 2)
            return 0
        lax.fori_loop(0, tm // ISSUE_GROUP, body, 0)

    @pl.when(i == 0)
    def _():
        issue(dcur_ref, 0)

    @pl.when(i + 1 < nsteps)
    def _():
        issue(dnext_ref, 1 - slot)

    pltpu.make_async_copy(y_hbm.at[pl.ds(0, SLAB * TOP_EXPERTS * tm)], ybuf.at[slot], sems.at[slot]).wait()
    lo = x_ref[:, :half]
    hi = x_ref[:, half:]
    for k in range(TOP_EXPERTS):
        a, b = _unpack_halves(_load_slab(ybuf.at[slot], k * tm, tm))
        g = gate_ref[:, k:k + 1]
        lo = lo + a * g
        hi = hi + b * g
    o_ref[:, :half] = lo
    o_ref[:, half:] = hi


def _moe_combine(x1, y_rows, dest, gates):
    n, d = x1.shape
    tm = min(COMBINE_ROWS, n)
    nt = n // tm
    per_step = tm * TOP_EXPERTS
    dest3 = dest.reshape(nt, 1, per_step)
    return pl.pallas_call(
        _combine_kernel,
        out_shape=jax.ShapeDtypeStruct((n, d), F32),
        grid=(nt,),
        in_specs=[pl.BlockSpec((1, 1, per_step), lambda i: (i, 0, 0), memory_space=pltpu.SMEM),
                  pl.BlockSpec((1, 1, per_step), lambda i: (jnp.minimum(i + 1, nt - 1), 0, 0),
                               memory_space=pltpu.SMEM),
                  pl.BlockSpec(memory_space=pl.ANY),
                  pl.BlockSpec((tm, d), lambda i: (i, 0)),
                  pl.BlockSpec((tm, LANES), lambda i: (i, 0))],
        out_specs=pl.BlockSpec((tm, d), lambda i: (i, 0)),
        scratch_shapes=[pltpu.VMEM((2, SLAB * per_step, LANES), jnp.uint32), pltpu.SemaphoreType.DMA((2,))],
        compiler_params=_params("arbitrary"),
        name="moe_combine",
    )(dest3, dest3, y_rows, x1, gates)


def kernel(x, positions, g_attn, w_in, g_qa, g_ka, g_idx_k, lambda_q1, lambda_k1, lambda_q2, lambda_k2, g_qb, g_kb, g_subln, w_o, g_ffn, w_router, b_router, w_gu, b_gu, w_down, b_down):
    b, s, d = x.shape
    n = b * s
    k_sel = min(TOPK_KEYS, s // 4)
    c_tab, s_tab = _rope_tables(positions)
    for layer in range(g_attn.shape[0]):
        lam_init = 0.8 - 0.6 * math.exp(-0.3 * layer)
        x2 = x.reshape(n, d)
        aq, iq, bq, bk, bv, kka, kki, v_t, iw_t = _in_proj(
            x2, g_attn[layer].reshape(1, d), w_in[layer], c_tab, s_tab,
            g_qa[layer], g_ka[layer], g_idx_k[layer], g_qb[layer], g_kb[layer])
        seq = lambda t: t.reshape(b, s, t.shape[-1])
        y_a = _dsa_attn(seq(aq), seq(iq), seq(kka), seq(kki), v_t, iw_t, k_sel)
        lam = (jnp.exp(jnp.sum(lambda_q1[layer].astype(F32) * lambda_k1[layer].astype(F32)))
               - jnp.exp(jnp.sum(lambda_q2[layer].astype(F32) * lambda_k2[layer].astype(F32)))
               + lam_init).reshape(1)
        y_b = _diff_attn(seq(bq), seq(bk), seq(bv), lam, g_subln[layer].reshape(1, B_V_DIM), lam_init)
        x1, hp, top_idx, gates, counts = _out_proj(
            x2, y_a.reshape(n, -1), y_b.reshape(n, -1), w_o[layer], g_ffn[layer].reshape(1, d),
            w_router[layer], b_router[layer])
        dest, block_expert, n_used, n_blocks = _moe_plan(
            top_idx, counts[:, 0, :N_EXPERTS], n // counts.shape[0])
        xs = _moe_dispatch(hp, dest, n_blocks * MOE_ROWS)
        y_rows = _moe_experts(xs, block_expert, n_used, w_gu[layer], b_gu[layer], w_down[layer], b_down[layer])
        x = _moe_combine(x1, y_rows, dest, gates).reshape(b, s, d)
    return x
```
